```python
import math
import jax, jax.numpy as jnp
from jax import lax
import numpy as np

D_MODEL = 2048
BATCH = 4
SEQ = 8192
DEPTH = 2

CHUNK = 64
Q_BLOCK = 128
EPS = 1e-6

GDN_HEADS = 6
GDN_DK = 128
GDN_DV = 128
GDN_CONV = 4
GDN_QK_W = GDN_HEADS * GDN_DK
GDN_V_W = GDN_HEADS * GDN_DV
ML_HEADS = 4
ML_DK = 128
ML_DV = 128
ML_QK_W = ML_HEADS * ML_DK
ML_V_W = ML_HEADS * ML_DV
SB_HEADS = 6
SB_DH = 128
SB_W = SB_HEADS * SB_DH
MEM_LEN = 256
XA_HEADS = 4
XA_DH = D_MODEL // XA_HEADS
D_FF = 5632
FFN_CONV = 3

IN_SIZES = (2 * GDN_QK_W + GDN_V_W,
            GDN_HEADS,
            GDN_HEADS,
            GDN_V_W,
            ML_QK_W, ML_QK_W, ML_V_W,
            ML_HEADS, ML_HEADS,
            ML_V_W,
            SB_W, SB_W, SB_W,
            3 * D_MODEL)
IN_WIDTH = sum(IN_SIZES)
BR_W = GDN_V_W + ML_V_W + SB_W

kernel_name = 'hybrid_gdn_mlstm_stickbreak_streaming_encoder'


def _split_cols(t, sizes):
    idx = np.cumsum(np.array(sizes))[:-1].tolist()
    return jnp.split(t, idx, axis=-1)


def rmsnorm(x, g):
    xf = x.astype(jnp.float32)
    y = xf * lax.rsqrt(jnp.mean(xf * xf, axis=-1, keepdims=True) + EPS)
    return (y * g.astype(jnp.float32)).astype(x.dtype)


def _l2norm(x):
    return x * lax.rsqrt(jnp.sum(x * x, axis=-1, keepdims=True) + EPS)


def causal_dwconv(x, w):
    K, C = w.shape
    return lax.conv_general_dilated(x, w.astype(x.dtype)[:, None, :], window_strides=(1,),
                                    padding=[(K - 1, 0)],
                                    dimension_numbers=('NWC', 'WIO', 'NWC'),
                                    feature_group_count=C)


def _to_chunks(t):
    B_, S_ = t.shape[:2]
    t = t.reshape((B_, S_ // CHUNK, CHUNK) + t.shape[2:])
    if t.ndim == 5:
        return t.transpose(1, 0, 3, 2, 4)
    return t.transpose(1, 0, 3, 2)


def _from_chunks(o):
    n, B_, H, C, d = o.shape
    return o.transpose(1, 0, 3, 2, 4).reshape(B_, n * C, H, d)


def gated_deltanet(q, k, v, g, beta):
    B_, S_, H, dk = q.shape
    dv = v.shape[-1]
    f32 = jnp.float32
    qc = _to_chunks(_l2norm(q.astype(f32)) * (dk ** -0.5))
    kc = _to_chunks(_l2norm(k.astype(f32)))
    vc = _to_chunks(v.astype(f32))
    gc = _to_chunks(g.astype(f32))
    bc = _to_chunks(beta.astype(f32))
    G = jnp.cumsum(gc, axis=-1)
    incl = jnp.tril(jnp.ones((CHUNK, CHUNK), dtype=bool))
    strict = jnp.tril(jnp.ones((CHUNK, CHUNK), dtype=bool), -1)
    decay = jnp.exp(jnp.where(incl, G[..., :, None] - G[..., None, :], -jnp.inf))
    kb = kc * bc[..., None]
    L = jnp.where(strict, jnp.einsum('nbhtd,nbhsd->nbhts', kb, kc) * decay, 0.0)
    eye = jnp.broadcast_to(jnp.eye(CHUNK, dtype=f32), L.shape)
    T = lax.linalg.triangular_solve(L, eye, left_side=True, lower=True, unit_diagonal=True)
    u = jnp.einsum('nbhts,nbhse->nbhte', T, vc * bc[..., None])
    w = jnp.einsum('nbhts,nbhsd->nbhtd', T, kb * jnp.exp(G)[..., None])
    a_intra = jnp.einsum('nbhtd,nbhsd->nbhts', qc, kc) * decay
    q_dec = qc * jnp.exp(G)[..., None]
    k_tail = kc * jnp.exp(G[..., -1:] - G)[..., None]
    g_last = jnp.exp(G[..., -1])

    def step(S, inp):
        u_i, w_i, a_i, qd_i, kt_i, gl_i = inp
        v_new = u_i - jnp.einsum('bhtd,bhde->bhte', w_i, S)
        o = jnp.einsum('bhtd,bhde->bhte', qd_i, S) + jnp.einsum('bhts,bhse->bhte', a_i, v_new)
        S = S * gl_i[..., None, None] + jnp.einsum('bhtd,bhte->bhde', kt_i, v_new)
        return S, o

    S0 = jnp.zeros((B_, H, dk, dv), f32)
    _, o = lax.scan(step, S0, (u, w, a_intra, q_dec, k_tail, g_last))
    return _from_chunks(o)


def mlstm(q, k, v, i_pre, f_pre):
    B_, S_, H, dk = q.shape
    dv = v.shape[-1]
    f32 = jnp.float32
    qc = _to_chunks(q.astype(f32) * (dk ** -0.5))
    kc = _to_chunks(k.astype(f32))
    vc = _to_chunks(v.astype(f32))
    ic = _to_chunks(i_pre.astype(f32))
    lfc = _to_chunks(jax.nn.log_sigmoid(f_pre.astype(f32)))
    b = jnp.cumsum(lfc, axis=-1)
    incl = jnp.tril(jnp.ones((CHUNK, CHUNK), dtype=bool))
    D = jnp.where(incl, b[..., :, None] - b[..., None, :] + ic[..., None, :], -jnp.inf)
    D_max = jnp.max(D, axis=-1)
    s_qk = jnp.einsum('nbhtd,nbhsd->nbhts', qc, kc)
    b_last = b[..., -1]
    a_end = b_last[..., None] - b + ic
    a_max = jnp.max(a_end, axis=-1)

    def step(carry, inp):
        C, nv, m = carry
        q_i, k_i, v_i, b_i, D_i, Dm_i, s_i, bl_i, ae_i, am_i = inp
        inter = b_i + m[..., None]
        m_t = jnp.maximum(inter, Dm_i)
        w_inter = jnp.exp(inter - m_t)
        W = jnp.exp(D_i - m_t[..., None]) * s_i
        num = (w_inter[..., None] * jnp.einsum('bhtd,bhde->bhte', q_i, C)
               + jnp.einsum('bhts,bhse->bhte', W, v_i))
        den = w_inter * jnp.einsum('bhtd,bhd->bht', q_i, nv) + jnp.sum(W, axis=-1)
        h = num / jnp.maximum(jnp.abs(den), jnp.exp(-m_t))[..., None]
        m_new = jnp.maximum(bl_i + m, am_i)
        w_old = jnp.exp(bl_i + m - m_new)
        w_s = jnp.exp(ae_i - m_new[..., None])
        C = w_old[..., None, None] * C + jnp.einsum('bhsd,bhse->bhde', k_i * w_s[..., None], v_i)
        nv = w_old[..., None] * nv + jnp.einsum('bhsd,bhs->bhd', k_i, w_s)
        return (C, nv, m_new), h

    init = (jnp.zeros((B_, H, dk, dv), f32), jnp.zeros((B_, H, dk), f32), jnp.zeros((B_, H), f32))
    _, h = lax.scan(step, init, (qc, kc, vc, b, D, D_max, s_qk, b_last, a_end, a_max))
    return _from_chunks(h)


def stick_breaking(q, k, v):
    B_, S_, H, dh = q.shape
    f32 = jnp.float32
    qh = (q.astype(f32) * (dh ** -0.5)).transpose(0, 2, 1, 3)
    kh = k.astype(f32).transpose(0, 2, 1, 3)
    vh = v.astype(f32).transpose(0, 2, 1, 3)
    outs = []
    for blk in range(S_ // Q_BLOCK):
        t0 = blk * Q_BLOCK
        t1 = t0 + Q_BLOCK
        logits = jnp.einsum('bhtd,bhsd->bhts', qh[:, :, t0:t1], kh[:, :, :t1])
        t_idx = t0 + jnp.arange(Q_BLOCK)
        s_idx = jnp.arange(t1)
        strict = s_idx[None, :] < t_idx[:, None]
        log_beta = jax.nn.log_sigmoid(logits)
        log_stay = jnp.where(strict, jax.nn.log_sigmoid(-logits), 0.0)
        log_surv = lax.cumsum(log_stay, axis=3, reverse=True) - log_stay
        A = jnp.where(strict, jnp.exp(log_beta + log_surv), 0.0)
        outs.append(jnp.einsum('bhts,bhsd->bhtd', A, vh[:, :, :t1]))
    o = jnp.concatenate(outs, axis=2)
    return o.transpose(0, 2, 1, 3)


def hybrid_mixer(h, w_in, conv_w, a_log, dt_bias, gdn_g, ml_bias, ml_g, w_br, w_out):
    B_, S_, _ = h.shape
    f32 = jnp.float32
    dt = h.dtype
    (qkv_a, a_pre, b_pre, z_a, q_b, k_b, v_b, i_b, f_b, o_b,
     q_c, k_c, v_c, gate_pre) = _split_cols(h @ w_in, IN_SIZES)

    def heads(t, n):
        return t.reshape(B_, S_, n, -1)

    qkv_a = jax.nn.silu(causal_dwconv(qkv_a, conv_w))
    q_a, k_a, v_a = _split_cols(qkv_a, (GDN_QK_W, GDN_QK_W, GDN_V_W))
    g_a = -jnp.exp(a_log.astype(f32)) * jax.nn.softplus(a_pre.astype(f32) + dt_bias.astype(f32))
    beta_a = jax.nn.sigmoid(b_pre.astype(f32))
    y_a = gated_deltanet(heads(q_a, GDN_HEADS), heads(k_a, GDN_HEADS), heads(v_a, GDN_HEADS), g_a, beta_a)
    y_a = rmsnorm(y_a, gdn_g) * jax.nn.silu(heads(z_a, GDN_HEADS).astype(f32))
    y_a = y_a.reshape(B_, S_, GDN_V_W).astype(dt)

    y_b = mlstm(heads(q_b, ML_HEADS), heads(k_b, ML_HEADS), heads(v_b, ML_HEADS),
                i_b + ml_bias[0], f_b + ml_bias[1])
    y_b = rmsnorm(y_b, ml_g) * jax.nn.sigmoid(heads(o_b, ML_HEADS).astype(f32))
    y_b = y_b.reshape(B_, S_, ML_V_W).astype(dt)

    y_c = stick_breaking(heads(q_c, SB_HEADS), heads(k_c, SB_HEADS), heads(v_c, SB_HEADS))
    y_c = y_c.reshape(B_, S_, SB_W).astype(dt)

    gate_a, gate_b, gate_c = jnp.split(jax.nn.sigmoid(gate_pre), 3, axis=-1)
    wb_a, wb_b, wb_c = jnp.split(w_br, [GDN_V_W, GDN_V_W + ML_V_W], axis=0)
    merged = gate_a * (y_a @ wb_a) + gate_b * (y_b @ wb_b) + gate_c * (y_c @ wb_c)
    return merged @ w_out


def memory_cross_attention(h, m, wq, wkv, wo, gq, gk):
    B_, S_, _ = h.shape
    M_ = m.shape[1]
    q = rmsnorm((h @ wq).reshape(B_, S_, XA_HEADS, XA_DH), gq)
    k, v = jnp.split(m @ wkv, 2, axis=-1)
    k = rmsnorm(k.reshape(B_, M_, XA_HEADS, XA_DH), gk)
    v = v.reshape(B_, M_, XA_HEADS, XA_DH)
    logits = jnp.einsum('bshd,bmhd->bhsm', q.astype(jnp.float32), k.astype(jnp.float32)) * (XA_DH ** -0.5)
    p = jax.nn.softmax(logits, axis=-1).astype(v.dtype)
    o = jnp.einsum('bhsm,bmhd->bshd', p, v).reshape(B_, S_, D_MODEL)
    return o @ wo


def conv_glu_ffn(h, w_up, w_conv, w_down):
    up = causal_dwconv(h @ w_up, w_conv)
    g, u = jnp.split(up, 2, axis=-1)
    return (jax.nn.silu(g) * u) @ w_down


def setup_inputs(seed: int = 0) -> dict:
    key = jax.random.key(seed)
    ks = jax.random.split(key, 26)
    f32 = jnp.float32

    def dense(k, fan_in, fan_out):
        return jax.random.normal(k, (DEPTH, fan_in, fan_out), f32) * (fan_in ** -0.5)

    def gain(k, n):
        return 1.0 + 0.02 * jax.random.normal(k, (DEPTH, n), f32)

    x = jax.random.normal(ks[0], (BATCH, SEQ, D_MODEL), f32)
    mem = jax.random.normal(ks[1], (BATCH, MEM_LEN, D_MODEL), f32)
    gdn_a_log = jnp.log(jax.random.uniform(ks[4], (DEPTH, GDN_HEADS), f32, 1.0, 16.0))
    dt0 = jnp.exp(jax.random.uniform(ks[5], (DEPTH, GDN_HEADS), f32, math.log(1e-3), math.log(1e-1)))
    gdn_dt_bias = dt0 + jnp.log(-jnp.expm1(-dt0))
    i_bias = 0.1 * jax.random.normal(ks[7], (DEPTH, ML_HEADS), f32)
    f_bias = jnp.linspace(3.0, 6.0, ML_HEADS, dtype=f32) + 0.1 * jax.random.normal(ks[8], (DEPTH, ML_HEADS), f32)
    ml_gate_bias = jnp.stack([i_bias, f_bias], axis=1)[..., None, None, :] if False else jnp.stack([i_bias, f_bias], axis=1)
    return {
        'x': x,
        'mem': mem,
        'norm_mix': gain(ks[2], D_MODEL),
        'w_in': dense(ks[3], D_MODEL, IN_WIDTH),
        'gdn_conv': jax.random.normal(ks[6], (DEPTH, GDN_CONV, 2 * GDN_QK_W + GDN_V_W), f32) * (GDN_CONV ** -0.5),
        'gdn_a_log': gdn_a_log,
        'gdn_dt_bias': gdn_dt_bias,
        'gdn_norm': gain(ks[9], GDN_DV),
        'ml_gate_bias': ml_gate_bias,
        'ml_norm': gain(ks[10], ML_DV),
        'w_br': dense(ks[11], BR_W, D_MODEL),
        'w_out': dense(ks[12], D_MODEL, D_MODEL),
        'norm_xa': gain(ks[13], D_MODEL),
        'norm_mem': gain(ks[14], D_MODEL),
        'xa_wq': dense(ks[15], D_MODEL, D_MODEL),
        'xa_wkv': dense(ks[16], D_MODEL, 2 * D_MODEL),
        'xa_wo': dense(ks[17], D_MODEL, D_MODEL),
        'xa_qnorm': gain(ks[18], XA_DH),
        'xa_knorm': gain(ks[19], XA_DH),
        'norm_ffn': gain(ks[20], D_MODEL),
        'ffn_up': dense(ks[21], D_MODEL, 2 * D_FF),
        'ffn_conv': jax.random.normal(ks[22], (DEPTH, FFN_CONV, 2 * D_FF), f32) * (FFN_CONV ** -0.5),
        'ffn_down': dense(ks[23], D_FF, D_MODEL),
    }


def reference(x, mem, norm_mix, w_in, gdn_conv, gdn_a_log, gdn_dt_bias, gdn_norm, ml_gate_bias,
              ml_norm, w_br, w_out, norm_xa, norm_mem, xa_wq, xa_wkv, xa_wo, xa_qnorm, xa_knorm,
              norm_ffn, ffn_up, ffn_conv, ffn_down):
    for l in range(DEPTH):
        x = x + hybrid_mixer(rmsnorm(x, norm_mix[l]), w_in[l], gdn_conv[l], gdn_a_log[l],
                             gdn_dt_bias[l], gdn_norm[l], ml_gate_bias[l], ml_norm[l],
                             w_br[l], w_out[l])
        x = x + memory_cross_attention(rmsnorm(x, norm_xa[l]), rmsnorm(mem, norm_mem[l]),
                                       xa_wq[l], xa_wkv[l], xa_wo[l], xa_qnorm[l], xa_knorm[l])
        x = x + conv_glu_ffn(rmsnorm(x, norm_ffn[l]), ffn_up[l], ffn_conv[l], ffn_down[l])
    return x
```

```python
import functools

import jax
import jax.numpy as jnp
from jax import lax
from jax.experimental import pallas as pl
from jax.experimental.pallas import tpu as pltpu

F32 = jnp.float32
BF16 = jnp.bfloat16

EPS = 1e-6
CHUNK = 64
D_MODEL = 2048
HEAD_DIM = 128
GDN_HEADS = 6
ML_HEADS = 4
SB_HEADS = 6
GDN_CONV = 4
XA_HEADS = 4
XA_DH = D_MODEL // XA_HEADS
D_FF = 5632
FFN_CONV = 3

GDN_W = GDN_HEADS * HEAD_DIM
ML_W = ML_HEADS * HEAD_DIM
SB_W = SB_HEADS * HEAD_DIM

OFF_GATE = 0
OFF_QKV_A = OFF_GATE + 3 * D_MODEL
OFF_Z_A = OFF_QKV_A + 3 * GDN_W
OFF_QB = OFF_Z_A + GDN_W
OFF_KB = OFF_QB + ML_W
OFF_VB = OFF_KB + ML_W
OFF_OB = OFF_VB + ML_W
OFF_QC = OFF_OB + ML_W
OFF_KC = OFF_QC + SB_W
OFF_VC = OFF_KC + SB_W
OFF_SMALL = OFF_VC + SB_W
SMALL_W = 256
PROJ_W = OFF_SMALL + SMALL_W
LANE_A = 0
LANE_BETA = LANE_A + GDN_HEADS
LANE_I = LANE_BETA + GDN_HEADS
LANE_F = LANE_I + ML_HEADS

SB_DEAD_LOG = -105.0

VMEM_LIMIT = 56 * 1024 * 1024


def _cparams(n_axes):
    return pltpu.CompilerParams(dimension_semantics=("arbitrary",) * n_axes,
                                vmem_limit_bytes=VMEM_LIMIT)


def _rms_rows(x, g):
    return x * lax.rsqrt(jnp.mean(x * x, axis=-1, keepdims=True) + EPS) * g


def _sigmoid(x):
    return 1.0 / (1.0 + jnp.exp(-x))


def _silu(x):
    return x * _sigmoid(x)


def _softplus(x):
    return jnp.maximum(x, 0.0) + jnp.log1p(jnp.exp(-jnp.abs(x)))


def _log_sigmoid(x):
    return -_softplus(-x)


def _split_bf16(x, parts):
    out = []
    r = x
    for i in range(parts):
        p = r.astype(BF16)
        out.append(p)
        if i + 1 < parts:
            r = r - p.astype(F32)
    return out


_NN = (((1,), (0,)), ((), ()))
_NT = (((1,), (1,)), ((), ()))
_TN = (((0,), (0,)), ((), ()))


def _dot(a, b, dims=_NN):
    return lax.dot_general(a.astype(BF16), b.astype(BF16), dims, preferred_element_type=F32)


def _dot_exact_b(a, b01, dims=_NN):
    acc = None
    for p in _split_bf16(a, 3):
        t = lax.dot_general(p, b01, dims, preferred_element_type=F32)
        acc = t if acc is None else acc + t
    return acc


def _dot_exact_a(a01, b, dims=_NN):
    acc = None
    for p in _split_bf16(b, 3):
        t = lax.dot_general(a01, p, dims, preferred_element_type=F32)
        acc = t if acc is None else acc + t
    return acc


def _dot_hi(a, b):
    a1, a2 = _split_bf16(a, 2)
    b1, b2 = _split_bf16(b, 2)
    d = functools.partial(lax.dot_general, dimension_numbers=_NN, preferred_element_type=F32)
    return d(a1, b1) + (d(a1, b2) + d(a2, b1))


def _causal_conv(x, prev8, w_ref, taps):
    row = lax.broadcasted_iota(jnp.int32, x.shape, 0)
    out = x * w_ref[taps - 1:taps, :]
    for d in range(1, taps):
        r = pltpu.roll(x, d, axis=0)
        for i in range(d):
            r = jnp.where(row == i, prev8[8 - d + i:8 - d + i + 1, :], r)
        out = out + r * w_ref[taps - 1 - d:taps - d, :]
    return out


def _tri_masks(n):
    r = lax.broadcasted_iota(jnp.int32, (n, n), 0)
    c = lax.broadcasted_iota(jnp.int32, (n, n), 1)
    return r, c


def _rms_matmul_kernel(x_ref, g_ref, w_ref, o_ref, xn_ref):
    @pl.when(pl.program_id(1) == 0)
    def _():
        xn_ref[...] = _rms_rows(x_ref[...], g_ref[...]).astype(BF16)

    o_ref[...] = jnp.dot(xn_ref[...], w_ref[...], preferred_element_type=F32).astype(o_ref.dtype)


def rms_matmul(x, g, w, *, tm, tn, out_dtype=F32):
    m, k = x.shape
    n = w.shape[1]
    assert m % tm == 0 and n % tn == 0
    return pl.pallas_call(
        _rms_matmul_kernel,
        grid=(m // tm, n // tn),
        in_specs=[pl.BlockSpec((tm, k), lambda i, j: (i, 0)),
                  pl.BlockSpec((1, k), lambda i, j: (0, 0)),
                  pl.BlockSpec((k, tn), lambda i, j: (0, j))],
        out_specs=pl.BlockSpec((tm, tn), lambda i, j: (i, j)),
        out_shape=jax.ShapeDtypeStruct((m, n), out_dtype),
        scratch_shapes=[pltpu.VMEM((tm, k), BF16)],
        compiler_params=_cparams(2),
        name="rms_matmul",
    )(x, g.reshape(1, k), w)


def _matmul_res_kernel(a_ref, w_ref, r_ref, o_ref):
    o_ref[...] = r_ref[...] + jnp.dot(a_ref[...], w_ref[...], preferred_element_type=F32)


def matmul_res(a, w, res, *, tm, tn):
    m, k = a.shape
    n = w.shape[1]
    assert m % tm == 0 and n % tn == 0
    return pl.pallas_call(
        _matmul_res_kernel,
        grid=(m // tm, n // tn),
        in_specs=[pl.BlockSpec((tm, k), lambda i, j: (i, 0)),
                  pl.BlockSpec((k, tn), lambda i, j: (0, j)),
                  pl.BlockSpec((tm, tn), lambda i, j: (i, j))],
        out_specs=pl.BlockSpec((tm, tn), lambda i, j: (i, j)),
        out_shape=jax.ShapeDtypeStruct((m, n), F32),
        compiler_params=_cparams(2),
        name="matmul_res",
    )(a, w, res)


def _gdn_kernel(q_ref, k_ref, v_ref, z_ref, sm_ref, cwq_ref, cwk_ref, cwv_ref, alog_ref, dtb_ref,
                gn_ref, o_ref,
                s_ref, pq_ref, pk_ref, pv_ref, qs_ref, ks_ref, vs_ref,
                u_ref, w_ref, qd_ref, kt_ref, a_ref, gl_ref):
    h = pl.program_id(1)
    tb = q_ref.shape[0]
    n_chunks = tb // CHUNK

    @pl.when(pl.program_id(2) == 0)
    def _():
        s_ref[...] = jnp.zeros_like(s_ref)
        pq_ref[...] = jnp.zeros_like(pq_ref)
        pk_ref[...] = jnp.zeros_like(pk_ref)
        pv_ref[...] = jnp.zeros_like(pv_ref)

    def conv_silu(x_ref, p_ref, cw_ref):
        x = x_ref[...]
        y = _silu(_causal_conv(x, p_ref[...], cw_ref, GDN_CONV))
        p_ref[...] = x[tb - 8:tb, :]
        return y

    q = conv_silu(q_ref, pq_ref, cwq_ref)
    k = conv_silu(k_ref, pk_ref, cwk_ref)
    vs_ref[...] = conv_silu(v_ref, pv_ref, cwv_ref)
    qs_ref[...] = q * lax.rsqrt(jnp.sum(q * q, axis=-1, keepdims=True) + EPS) * (HEAD_DIM ** -0.5)
    ks_ref[...] = k * lax.rsqrt(jnp.sum(k * k, axis=-1, keepdims=True) + EPS)

    r, c = _tri_masks(CHUNK)
    incl = c <= r
    strict = c < r
    tril01 = jnp.where(incl, 1.0, 0.0).astype(BF16)
    eye = jnp.where(r == c, 1.0, 0.0).astype(F32)
    lane = lax.broadcasted_iota(jnp.int32, (1, SMALL_W), 1)
    oh_a = jnp.where(lane == LANE_A + h, 1.0, 0.0).astype(F32)
    oh_b = jnp.where(lane == LANE_BETA + h, 1.0, 0.0).astype(F32)
    oh_a8 = jnp.broadcast_to(oh_a, (8, SMALL_W)).astype(BF16)
    neg_a = -jnp.exp(alog_ref[...])
    dtb = dtb_ref[...]

    def prep(ci, carry):
        r0 = pl.multiple_of(ci * CHUNK, CHUNK)
        rows = pl.ds(r0, CHUNK)
        qc = qs_ref[rows, :]
        kc = ks_ref[rows, :]
        vc = vs_ref[rows, :]
        sm = sm_ref[rows, :]
        g_full = neg_a * _softplus(sm + dtb)
        cum = _dot_exact_a(tril01, g_full)
        g_col = jnp.sum(cum * oh_a, axis=-1, keepdims=True)
        g_row = _dot_exact_a(oh_a8, cum, _NT)[0:1, :]
        beta = jnp.sum(_sigmoid(sm) * oh_b, axis=-1, keepdims=True)
        decay = jnp.exp(jnp.where(incl, g_col - g_row, -jnp.inf))
        kb = kc * beta
        lmat = jnp.where(strict, _dot(kb, kc, _NT) * decay, 0.0)
        p = -lmat
        t = eye + p
        for _ in range(5):
            p = _dot_hi(p, p)
            t = t + _dot_hi(t, p)
        eg = jnp.exp(g_col)
        g_last = g_col[CHUNK - 1:CHUNK, :]
        u_ref[rows, :] = _dot(t, vc * beta)
        w_ref[rows, :] = _dot(t, kb * eg)
        a_ref[rows, :] = _dot(qc, kc, _NT) * decay
        qd_ref[rows, :] = qc * eg
        kt_ref[rows, :] = kc * jnp.exp(g_last - g_col)
        gl_ref[pl.ds(pl.multiple_of(ci * 8, 8), 8), :] = jnp.broadcast_to(jnp.exp(g_last), (8, HEAD_DIM))
        return carry

    lax.fori_loop(0, n_chunks, prep, 0)

    gn = gn_ref[...]

    def scan(ci, carry):
        r0 = pl.multiple_of(ci * CHUNK, CHUNK)
        rows = pl.ds(r0, CHUNK)
        s = s_ref[...]
        sb = s.astype(BF16)
        v_new = u_ref[rows, :] - _dot(w_ref[rows, :], sb)
        vb = v_new.astype(BF16)
        o = _dot(qd_ref[rows, :], sb) + _dot(a_ref[rows, :], vb)
        gl = gl_ref[pl.ds(pl.multiple_of(ci * 8, 8), 1), :]
        s_ref[...] = s * gl + _dot(kt_ref[rows, :], vb, _TN)
        y = _rms_rows(o, gn) * _silu(z_ref[rows, :])
        o_ref[rows, :] = y.astype(o_ref.dtype)
        return carry

    lax.fori_loop(0, n_chunks, scan, 0)


def gdn_mixer(proj, conv_w, alog_vec, dtb_vec, gnorm, *, tb):
    b, s, _ = proj.shape
    assert s % tb == 0 and tb % CHUNK == 0
    hd = HEAD_DIM
    col = lambda off: (lambda bi, hi, ti: (bi, ti, off // hd + hi))
    cwcol = lambda off: (lambda bi, hi, ti: (0, off // hd + hi))
    const = lambda bi, hi, ti: (0, 0)
    blk = (None, tb, hd)
    return pl.pallas_call(
        _gdn_kernel,
        grid=(b, GDN_HEADS, s // tb),
        in_specs=[pl.BlockSpec(blk, col(OFF_QKV_A)),
                  pl.BlockSpec(blk, col(OFF_QKV_A + GDN_W)),
                  pl.BlockSpec(blk, col(OFF_QKV_A + 2 * GDN_W)),
                  pl.BlockSpec(blk, col(OFF_Z_A)),
                  pl.BlockSpec((None, tb, SMALL_W), lambda bi, hi, ti: (bi, ti, OFF_SMALL // SMALL_W)),
                  pl.BlockSpec((GDN_CONV, hd), cwcol(0)),
                  pl.BlockSpec((GDN_CONV, hd), cwcol(GDN_W)),
                  pl.BlockSpec((GDN_CONV, hd), cwcol(2 * GDN_W)),
                  pl.BlockSpec((1, SMALL_W), const),
                  pl.BlockSpec((1, SMALL_W), const),
                  pl.BlockSpec((1, hd), const)],
        out_specs=pl.BlockSpec(blk, lambda bi, hi, ti: (bi, ti, hi)),
        out_shape=jax.ShapeDtypeStruct((b, s, GDN_W), BF16),
        scratch_shapes=[pltpu.VMEM((hd, hd), F32)]
                       + [pltpu.VMEM((8, hd), F32)] * 3
                       + [pltpu.VMEM((tb, hd), F32)] * 3
                       + [pltpu.VMEM((tb, hd), F32)] * 4
                       + [pltpu.VMEM((tb, CHUNK), F32),
                          pltpu.VMEM((tb // CHUNK * 8, hd), F32)],
        compiler_params=_cparams(3),
        name="gdn_mixer",
    )(proj, proj, proj, proj, proj, conv_w, conv_w, conv_w, alog_vec, dtb_vec, gnorm.reshape(1, hd))


def _mlstm_kernel(q_ref, k_ref, v_ref, og_ref, sm_ref, ib_ref, fb_ref, gn_ref, o_ref,
                  c_ref, n_ref, m_ref):
    h = pl.program_id(1)
    tb = q_ref.shape[0]
    n_chunks = tb // CHUNK

    @pl.when(pl.program_id(2) == 0)
    def _():
        c_ref[...] = jnp.zeros_like(c_ref)
        n_ref[...] = jnp.zeros_like(n_ref)
        m_ref[...] = jnp.zeros_like(m_ref)

    r, c = _tri_masks(CHUNK)
    incl = c <= r
    tril01 = jnp.where(incl, 1.0, 0.0).astype(BF16)
    lane = lax.broadcasted_iota(jnp.int32, (1, SMALL_W), 1)
    oh_i = jnp.where(lane == LANE_I + h, 1.0, 0.0).astype(F32)
    oh_f = jnp.where(lane == LANE_F + h, 1.0, 0.0).astype(F32)
    oh_i8 = jnp.broadcast_to(oh_i, (8, SMALL_W)).astype(BF16)
    oh_f8 = jnp.broadcast_to(oh_f, (8, SMALL_W)).astype(BF16)
    ib = ib_ref[...]
    fb = fb_ref[...]
    gn = gn_ref[...]

    def chunk(ci, carry):
        r0 = pl.multiple_of(ci * CHUNK, CHUNK)
        rows = pl.ds(r0, CHUNK)
        qc = q_ref[rows, :] * (HEAD_DIM ** -0.5)
        kc = k_ref[rows, :]
        vc = v_ref[rows, :]
        sm = sm_ref[rows, :]
        i_full = sm + ib
        b_full = _dot_exact_a(tril01, _log_sigmoid(sm + fb))
        b_col = jnp.sum(b_full * oh_f, axis=-1, keepdims=True)
        i_col = jnp.sum(i_full * oh_i, axis=-1, keepdims=True)
        b_row = _dot_exact_a(oh_f8, b_full, _NT)[0:1, :]
        i_row = _dot_exact_a(oh_i8, i_full, _NT)[0:1, :]
        dmat = jnp.where(incl, b_col - b_row + i_row, -jnp.inf)
        d_max = jnp.max(dmat, axis=-1, keepdims=True)
        s_qk = _dot(qc, kc, _NT)
        b_last = b_col[CHUNK - 1:CHUNK, :]
        a_end = b_last - b_col + i_col
        a_max = jnp.max(a_end, axis=0, keepdims=True)

        cmat = c_ref[...]
        nvec = n_ref[...]
        m = m_ref[0:1, 0:1]
        inter = b_col + m
        m_t = jnp.maximum(inter, d_max)
        w_inter = jnp.exp(inter - m_t)
        wmat = jnp.exp(dmat - m_t) * s_qk
        num = w_inter * _dot(qc, cmat) + _dot(wmat, vc)
        den = (w_inter * jnp.sum(qc * nvec, axis=-1, keepdims=True)
               + jnp.sum(wmat, axis=-1, keepdims=True))
        hout = num / jnp.maximum(jnp.abs(den), jnp.exp(-m_t))
        m_new = jnp.maximum(b_last + m, a_max)
        w_old = jnp.exp(b_last + m - m_new)
        kw = kc * jnp.exp(a_end - m_new)
        c_ref[...] = w_old * cmat + _dot(kw, vc, _TN)
        n_ref[...] = w_old * nvec + jnp.sum(kw, axis=0, keepdims=True)
        m_ref[...] = jnp.broadcast_to(m_new, m_ref.shape)
        y = _rms_rows(hout, gn) * _sigmoid(og_ref[rows, :])
        o_ref[rows, :] = y.astype(o_ref.dtype)
        return carry

    lax.fori_loop(0, n_chunks, chunk, 0)


def mlstm_mixer(proj, ib_vec, fb_vec, gnorm, *, tb):
    b, s, _ = proj.shape
    assert s % tb == 0 and tb % CHUNK == 0
    hd = HEAD_DIM
    col = lambda off: (lambda bi, hi, ti: (bi, ti, off // hd + hi))
    const = lambda bi, hi, ti: (0, 0)
    blk = (None, tb, hd)
    return pl.pallas_call(
        _mlstm_kernel,
        grid=(b, ML_HEADS, s // tb),
        in_specs=[pl.BlockSpec(blk, col(OFF_QB)),
                  pl.BlockSpec(blk, col(OFF_KB)),
                  pl.BlockSpec(blk, col(OFF_VB)),
                  pl.BlockSpec(blk, col(OFF_OB)),
                  pl.BlockSpec((None, tb, SMALL_W), lambda bi, hi, ti: (bi, ti, OFF_SMALL // SMALL_W)),
                  pl.BlockSpec((1, SMALL_W), const),
                  pl.BlockSpec((1, SMALL_W), const),
                  pl.BlockSpec((1, hd), const)],
        out_specs=pl.BlockSpec(blk, lambda bi, hi, ti: (bi, ti, hi)),
        out_shape=jax.ShapeDtypeStruct((b, s, ML_W), BF16),
        scratch_shapes=[pltpu.VMEM((hd, hd), F32),
                        pltpu.VMEM((1, hd), F32),
                        pltpu.VMEM((8, hd), F32)],
        compiler_params=_cparams(3),
        name="mlstm_mixer",
    )(proj, proj, proj, proj, proj, ib_vec, fb_vec, gnorm.reshape(1, hd))


def _sb_kernel(q_ref, k_ref, v_ref, o_ref, acc_ref, carry_ref):
    tq = q_ref.shape[0]
    qi = pl.program_id(2)
    qb = (q_ref[...] * (HEAD_DIM ** -0.5)).astype(BF16)
    r, c = _tri_masks(tq)
    strict = c < r
    later01 = jnp.where(r > c, 1.0, 0.0).astype(BF16)

    acc_ref[...] = jnp.zeros_like(acc_ref)
    carry_ref[...] = jnp.zeros_like(carry_ref)

    def process(kb, diag):
        rows = pl.ds(pl.multiple_of(kb * tq, tq), tq)
        z = lax.dot_general(qb, k_ref[rows, :].astype(BF16), _NT, preferred_element_type=F32)
        sp = jnp.log1p(jnp.exp(-jnp.abs(z)))
        log_beta = jnp.minimum(z, 0.0) - sp
        log_stay = jnp.minimum(-z, 0.0) - sp
        if diag:
            log_stay = jnp.where(strict, log_stay, 0.0)
        hi, lo = _split_bf16(log_stay, 2)
        inblk = (lax.dot_general(hi, later01, _NN, preferred_element_type=F32)
                 + lax.dot_general(lo, later01, _NN, preferred_element_type=F32))
        carry = carry_ref[...]
        a = jnp.exp(log_beta + inblk + carry)
        if diag:
            a = jnp.where(strict, a, 0.0)
        acc_ref[...] += _dot(a, v_ref[rows, :])
        new_carry = carry + inblk[:, 0:1] + log_stay[:, 0:1]
        carry_ref[...] = new_carry
        return jnp.max(new_carry)

    top = process(qi, True)

    def cond(st):
        return jnp.logical_and(st[0] >= 0, st[1] >= SB_DEAD_LOG)

    def body(st):
        return st[0] - 1, process(st[0], False)

    lax.while_loop(cond, body, (qi - 1, top))
    o_ref[...] = acc_ref[...].astype(o_ref.dtype)


def sb_mixer(proj, *, tq):
    b, s, _ = proj.shape
    assert s % tq == 0
    hd = HEAD_DIM
    return pl.pallas_call(
        _sb_kernel,
        grid=(b, SB_HEADS, s // tq),
        in_specs=[pl.BlockSpec((None, tq, hd), lambda bi, hi, ti: (bi, ti, OFF_QC // hd + hi)),
                  pl.BlockSpec((None, s, hd), lambda bi, hi, ti: (bi, 0, OFF_KC // hd + hi)),
                  pl.BlockSpec((None, s, hd), lambda bi, hi, ti: (bi, 0, OFF_VC // hd + hi))],
        out_specs=pl.BlockSpec((None, tq, hd), lambda bi, hi, ti: (bi, ti, hi)),
        out_shape=jax.ShapeDtypeStruct((b, s, SB_W), BF16),
        scratch_shapes=[pltpu.VMEM((tq, hd), F32), pltpu.VMEM((tq, 1), F32)],
        compiler_params=_cparams(3),
        name="sb_mixer",
    )(proj, proj, proj)


def _merge_kernel(ya_ref, yb_ref, yc_ref, wa_ref, wb_ref, wc_ref, ga_ref, gb_ref, gc_ref, o_ref):
    d = functools.partial(jnp.dot, preferred_element_type=F32)
    m = (_sigmoid(ga_ref[...]) * d(ya_ref[...], wa_ref[...])
         + _sigmoid(gb_ref[...]) * d(yb_ref[...], wb_ref[...])
         + _sigmoid(gc_ref[...]) * d(yc_ref[...], wc_ref[...]))
    o_ref[...] = m.astype(o_ref.dtype)


def merge_branches(ya, yb, yc, wa, wb, wc, proj2d, *, tm, tn):
    m = ya.shape[0]
    n = wa.shape[1]
    assert m % tm == 0 and n % tn == 0
    gate = lambda which: (lambda i, j: (i, (OFF_GATE + which * D_MODEL) // tn + j))
    row = lambda i, j: (i, 0)
    wcol = lambda i, j: (0, j)
    return pl.pallas_call(
        _merge_kernel,
        grid=(m // tm, n // tn),
        in_specs=[pl.BlockSpec((tm, ya.shape[1]), row),
                  pl.BlockSpec((tm, yb.shape[1]), row),
                  pl.BlockSpec((tm, yc.shape[1]), row),
                  pl.BlockSpec((wa.shape[0], tn), wcol),
                  pl.BlockSpec((wb.shape[0], tn), wcol),
                  pl.BlockSpec((wc.shape[0], tn), wcol),
                  pl.BlockSpec((tm, tn), gate(0)),
                  pl.BlockSpec((tm, tn), gate(1)),
                  pl.BlockSpec((tm, tn), gate(2))],
        out_specs=pl.BlockSpec((tm, tn), lambda i, j: (i, j)),
        out_shape=jax.ShapeDtypeStruct((m, n), BF16),
        compiler_params=_cparams(2),
        name="merge_branches",
    )(ya, yb, yc, wa, wb, wc, proj2d, proj2d, proj2d)


def _xattn_kernel(q_ref, kv_ref, gq_ref, gk_ref, o_ref):
    gq = gq_ref[...]
    gk = gk_ref[...]
    for hh in range(XA_HEADS):
        cs = slice(hh * XA_DH, (hh + 1) * XA_DH)
        qh = _rms_rows(q_ref[:, cs], gq)
        kh = _rms_rows(kv_ref[:, cs], gk)
        vh = kv_ref[:, D_MODEL + hh * XA_DH:D_MODEL + (hh + 1) * XA_DH]
        logits = _dot(qh, kh, _NT) * (XA_DH ** -0.5)
        mx = jnp.max(logits, axis=-1, keepdims=True)
        e = jnp.exp(logits - mx)
        p = e / jnp.sum(e, axis=-1, keepdims=True)
        o_ref[:, cs] = _dot(p, vh).astype(o_ref.dtype)


def cross_attention(q, kv, gq, gk, *, tm):
    b, s, d = q.shape
    mlen = kv.shape[1]
    assert s % tm == 0
    const = lambda bi, ti: (0, 0)
    return pl.pallas_call(
        _xattn_kernel,
        grid=(b, s // tm),
        in_specs=[pl.BlockSpec((None, tm, d), lambda bi, ti: (bi, ti, 0)),
                  pl.BlockSpec((None, mlen, 2 * d), lambda bi, ti: (bi, 0, 0)),
                  pl.BlockSpec((1, XA_DH), const),
                  pl.BlockSpec((1, XA_DH), const)],
        out_specs=pl.BlockSpec((None, tm, d), lambda bi, ti: (bi, ti, 0)),
        out_shape=jax.ShapeDtypeStruct((b, s, d), BF16),
        compiler_params=_cparams(2),
        name="cross_attention",
    )(q, kv, gq.reshape(1, XA_DH), gk.reshape(1, XA_DH))


def _ffn_up_kernel(x_ref, gn_ref, wg_ref, wu_ref, cg_ref, cu_ref, o_ref,
                   xn_ref, pg_ref, pu_ref, *, blocks_per_seq):
    i = pl.program_id(0)
    j = pl.program_id(1)
    tm = x_ref.shape[0]

    @pl.when(j == 0)
    def _():
        xn_ref[...] = _rms_rows(x_ref[...], gn_ref[...]).astype(BF16)

    @pl.when(i % blocks_per_seq == 0)
    def _():
        pg_ref[j] = jnp.zeros(pg_ref.shape[1:], F32)
        pu_ref[j] = jnp.zeros(pu_ref.shape[1:], F32)

    xn = xn_ref[...]
    g = jnp.dot(xn, wg_ref[...], preferred_element_type=F32)
    u = jnp.dot(xn, wu_ref[...], preferred_element_type=F32)
    gc = _causal_conv(g, pg_ref[j], cg_ref, FFN_CONV)
    uc = _causal_conv(u, pu_ref[j], cu_ref, FFN_CONV)
    pg_ref[j] = g[tm - 8:tm, :]
    pu_ref[j] = u[tm - 8:tm, :]
    o_ref[...] = (_silu(gc) * uc).astype(o_ref.dtype)


def ffn_up_glu(x, gn, w_up, w_conv, *, tm, tn, seq_len):
    m, k = x.shape
    assert m % tm == 0 and D_FF % tn == 0 and seq_len % tm == 0
    nj = D_FF // tn
    kern = functools.partial(_ffn_up_kernel, blocks_per_seq=seq_len // tm)
    return pl.pallas_call(
        kern,
        grid=(m // tm, nj),
        in_specs=[pl.BlockSpec((tm, k), lambda i, j: (i, 0)),
                  pl.BlockSpec((1, k), lambda i, j: (0, 0)),
                  pl.BlockSpec((k, tn), lambda i, j: (0, j)),
                  pl.BlockSpec((k, tn), lambda i, j: (0, nj + j)),
                  pl.BlockSpec((FFN_CONV, tn), lambda i, j: (0, j)),
                  pl.BlockSpec((FFN_CONV, tn), lambda i, j: (0, nj + j))],
        out_specs=pl.BlockSpec((tm, tn), lambda i, j: (i, j)),
        out_shape=jax.ShapeDtypeStruct((m, D_FF), BF16),
        scratch_shapes=[pltpu.VMEM((tm, k), BF16),
                        pltpu.VMEM((nj, 8, tn), F32),
                        pltpu.VMEM((nj, 8, tn), F32)],
        compiler_params=_cparams(2),
        name="ffn_up",
    )(x, gn.reshape(1, k), w_up, w_up, w_conv, w_conv)


def _pad_lanes(v, lane0, width):
    out = jnp.zeros((1, width), F32)
    return lax.dynamic_update_slice(out, v.reshape(1, -1).astype(F32), (0, lane0))


def _arrange_w_in(w):
    sizes = (3 * GDN_W, GDN_HEADS, GDN_HEADS, GDN_W, ML_W, ML_W, ML_W, ML_HEADS, ML_HEADS, ML_W,
             SB_W, SB_W, SB_W, 3 * D_MODEL)
    offs = [0]
    for sz in sizes:
        offs.append(offs[-1] + sz)
    part = lambda i: w[:, offs[i]:offs[i + 1]]
    (qkv_a, a_pre, b_pre, z_a, q_b, k_b, v_b, i_b, f_b, o_b, q_c, k_c, v_c, gate) = (
        part(i) for i in range(len(sizes)))
    small = jnp.concatenate([a_pre, b_pre, i_b, f_b], axis=1)
    small = jnp.pad(small, ((0, 0), (0, SMALL_W - small.shape[1])))
    return jnp.concatenate([gate, qkv_a, z_a, q_b, k_b, v_b, o_b, q_c, k_c, v_c, small],
                           axis=1).astype(BF16)


def _layer(x, mem2d, p, *, batch, seq, mem_len):
    n = batch * seq
    tm = min(512, seq)
    proj = rms_matmul(x, p["norm_mix"], p["w_in"], tm=tm, tn=512)
    proj3 = proj.reshape(batch, seq, PROJ_W)
    tb = min(1024, seq)
    ya = gdn_mixer(proj3, p["gdn_conv"], p["alog_vec"], p["dtb_vec"], p["gdn_norm"], tb=tb)
    yb = mlstm_mixer(proj3, p["ib_vec"], p["fb_vec"], p["ml_norm"], tb=tb)
    yc = sb_mixer(proj3, tq=128)
    merged = merge_branches(ya.reshape(n, GDN_W), yb.reshape(n, ML_W), yc.reshape(n, SB_W),
                            p["wb_a"], p["wb_b"], p["wb_c"], proj, tm=tm, tn=512)
    x = matmul_res(merged, p["w_out"], x, tm=tm, tn=512)
    q = rms_matmul(x, p["norm_xa"], p["xa_wq"], tm=tm, tn=512)
    kv = rms_matmul(mem2d, p["norm_mem"], p["xa_wkv"], tm=min(512, mem2d.shape[0]), tn=512)
    o = cross_attention(q.reshape(batch, seq, D_MODEL), kv.reshape(batch, mem_len, 2 * D_MODEL),
                        p["xa_qnorm"], p["xa_knorm"], tm=tm)
    x = matmul_res(o.reshape(n, D_MODEL), p["xa_wo"], x, tm=tm, tn=512)
    act = ffn_up_glu(x, p["norm_ffn"], p["ffn_up"], p["ffn_conv"], tm=tm, tn=512, seq_len=seq)
    x = matmul_res(act, p["ffn_down"], x, tm=tm, tn=512)
    return x


def kernel(x, mem, norm_mix, w_in, gdn_conv, gdn_a_log, gdn_dt_bias, gdn_norm, ml_gate_bias, ml_norm, w_br, w_out, norm_xa, norm_mem, xa_wq, xa_wkv, xa_wo, xa_qnorm, xa_knorm, norm_ffn, ffn_up, ffn_conv, ffn_down):
    batch, seq, d = x.shape
    mem_len = mem.shape[1]
    depth = w_in.shape[0]
    h = x.reshape(batch * seq, d)
    mem2d = mem.reshape(batch * mem_len, d)
    for l in range(depth):
        p = {
            "norm_mix": norm_mix[l],
            "w_in": _arrange_w_in(w_in[l]),
            "gdn_conv": gdn_conv[l],
            "alog_vec": _pad_lanes(gdn_a_log[l], LANE_A, SMALL_W),
            "dtb_vec": _pad_lanes(gdn_dt_bias[l], LANE_A, SMALL_W),
            "gdn_norm": gdn_norm[l],
            "ib_vec": _pad_lanes(ml_gate_bias[l, 0], LANE_I, SMALL_W),
            "fb_vec": _pad_lanes(ml_gate_bias[l, 1], LANE_F, SMALL_W),
            "ml_norm": ml_norm[l],
            "wb_a": w_br[l, :GDN_W].astype(BF16),
            "wb_b": w_br[l, GDN_W:GDN_W + ML_W].astype(BF16),
            "wb_c": w_br[l, GDN_W + ML_W:].astype(BF16),
            "w_out": w_out[l].astype(BF16),
            "norm_xa": norm_xa[l],
            "norm_mem": norm_mem[l],
            "xa_wq": xa_wq[l].astype(BF16),
            "xa_wkv": xa_wkv[l].astype(BF16),
            "xa_wo": xa_wo[l].astype(BF16),
            "xa_qnorm": xa_qnorm[l],
            "xa_knorm": xa_knorm[l],
            "norm_ffn": norm_ffn[l],
            "ffn_up": ffn_up[l].astype(BF16),
            "ffn_conv": ffn_conv[l],
            "ffn_down": ffn_down[l].astype(BF16),
        }
        h = _layer(h, mem2d, p, batch=batch, seq=seq, mem_len=mem_len)
    return h.reshape(batch, seq, d)
```

```python
import functools

import jax
import jax.numpy as jnp
from jax import lax
from jax.experimental import pallas as pl
from jax.experimental.pallas import tpu as pltpu

F32 = jnp.float32
BF16 = jnp.bfloat16

EPS = 1e-6
CHUNK = 64
D_MODEL = 2048
HEAD_DIM = 128
GDN_HEADS = 6
ML_HEADS = 4
SB_HEADS = 6
GDN_CONV = 4
GDN_PREP_CHUNKS = 2
XA_HEADS = 4
XA_DH = D_MODEL // XA_HEADS
D_FF = 5632
FFN_CONV = 3

GDN_W = GDN_HEADS * HEAD_DIM
ML_W = ML_HEADS * HEAD_DIM
SB_W = SB_HEADS * HEAD_DIM

OFF_GATE = 0
OFF_QKV_A = OFF_GATE + 3 * D_MODEL
OFF_Z_A = OFF_QKV_A + 3 * GDN_W
OFF_QB = OFF_Z_A + GDN_W
OFF_KB = OFF_QB + ML_W
OFF_VB = OFF_KB + ML_W
OFF_OB = OFF_VB + ML_W
OFF_QC = OFF_OB + ML_W
OFF_KC = OFF_QC + SB_W
OFF_VC = OFF_KC + SB_W
OFF_SMALL = OFF_VC + SB_W
SMALL_W = 256
PROJ_W = OFF_SMALL + SMALL_W
LANE_A = 0
LANE_BETA = LANE_A + GDN_HEADS
LANE_I = LANE_BETA + GDN_HEADS
LANE_F = LANE_I + ML_HEADS

SB_DEAD_LOG = -105.0

VMEM_LIMIT = 56 * 1024 * 1024


def _cparams(n_axes):
    return pltpu.CompilerParams(dimension_semantics=("arbitrary",) * n_axes,
                                vmem_limit_bytes=VMEM_LIMIT)


def _rms_rows(x, g):
    return x * lax.rsqrt(jnp.mean(x * x, axis=-1, keepdims=True) + EPS) * g


def _sigmoid(x):
    return 1.0 / (1.0 + jnp.exp(-x))


def _silu(x):
    return x * _sigmoid(x)


def _softplus(x):
    return jnp.maximum(x, 0.0) + jnp.log1p(jnp.exp(-jnp.abs(x)))


def _log_sigmoid(x):
    return -_softplus(-x)


def _split_bf16(x, parts):
    out = []
    r = x
    for i in range(parts):
        p = r.astype(BF16)
        out.append(p)
        if i + 1 < parts:
            r = r - p.astype(F32)
    return out


_NN = (((1,), (0,)), ((), ()))
_NT = (((1,), (1,)), ((), ()))
_TN = (((0,), (0,)), ((), ()))


def _dot(a, b, dims=_NN):
    return lax.dot_general(a.astype(BF16), b.astype(BF16), dims, preferred_element_type=F32)


def _dot_exact_b(a, b01, dims=_NN):
    acc = None
    for p in _split_bf16(a, 3):
        t = lax.dot_general(p, b01, dims, preferred_element_type=F32)
        acc = t if acc is None else acc + t
    return acc


def _dot_exact_a(a01, b, dims=_NN):
    acc = None
    for p in _split_bf16(b, 3):
        t = lax.dot_general(a01, p, dims, preferred_element_type=F32)
        acc = t if acc is None else acc + t
    return acc


def _dot_hi_many(a_list, b_list):
    sa = [_split_bf16(a, 2) for a in a_list]
    sb = [_split_bf16(b, 2) for b in b_list]
    d = functools.partial(lax.dot_general, dimension_numbers=_NN, preferred_element_type=F32)
    hh = [d(a[0], b[0]) for a, b in zip(sa, sb)]
    hl = [d(a[0], b[1]) for a, b in zip(sa, sb)]
    lh = [d(a[1], b[0]) for a, b in zip(sa, sb)]
    return [x + (y + z) for x, y, z in zip(hh, hl, lh)]


def _causal_conv(x, prev8, w_ref, taps):
    def taps_sum(y):
        out = y * w_ref[taps - 1:taps, :]
        for d in range(1, taps):
            out = out + pltpu.roll(y, d, axis=0) * w_ref[taps - 1 - d:taps - d, :]
        return out

    head = taps_sum(jnp.concatenate([prev8, x[0:8, :]], axis=0))[8:16, :]
    return jnp.concatenate([head, taps_sum(x)[8:, :]], axis=0)


def _tri_masks(n):
    r = lax.broadcasted_iota(jnp.int32, (n, n), 0)
    c = lax.broadcasted_iota(jnp.int32, (n, n), 1)
    return r, c


def _rms_matmul_kernel(x_ref, g_ref, w_ref, o_ref, xn_ref, *, w_dims):
    @pl.when(pl.program_id(1) == 0)
    def _():
        xn_ref[...] = _rms_rows(x_ref[...], g_ref[...]).astype(BF16)

    o_ref[...] = lax.dot_general(xn_ref[...], w_ref[...], w_dims,
                                 preferred_element_type=F32).astype(o_ref.dtype)


def rms_matmul(x, g, w, *, tm, tn, w_transposed=False):
    m, k = x.shape
    n = w.shape[0] if w_transposed else w.shape[1]
    assert m % tm == 0 and n % tn == 0
    w_mode = dict(pipeline_mode=pl.Buffered(1)) if tn == n else {}
    if w_transposed:
        w_spec = pl.BlockSpec((tn, k), lambda i, j: (j, 0), **w_mode)
    else:
        w_spec = pl.BlockSpec((k, tn), lambda i, j: (0, j), **w_mode)
    return pl.pallas_call(
        functools.partial(_rms_matmul_kernel, w_dims=_NT if w_transposed else _NN),
        grid=(m // tm, n // tn),
        in_specs=[pl.BlockSpec((tm, k), lambda i, j: (i, 0)),
                  pl.BlockSpec((1, k), lambda i, j: (0, 0)),
                  w_spec],
        out_specs=pl.BlockSpec((tm, tn), lambda i, j: (i, j)),
        out_shape=jax.ShapeDtypeStruct((m, n), F32),
        scratch_shapes=[pltpu.VMEM((tm, k), BF16)],
        compiler_params=_cparams(2),
        name="rms_matmul",
    )(x, g.reshape(1, k), w)


def _matmul_res_kernel(a_ref, w_ref, r_ref, o_ref):
    o_ref[...] = r_ref[...] + jnp.dot(a_ref[...], w_ref[...], preferred_element_type=F32)


def matmul_res(a, w, res, *, tm, tn):
    m, k = a.shape
    n = w.shape[1]
    assert m % tm == 0 and n % tn == 0
    w_mode = dict(pipeline_mode=pl.Buffered(1)) if tn == n else {}
    return pl.pallas_call(
        _matmul_res_kernel,
        grid=(m // tm, n // tn),
        in_specs=[pl.BlockSpec((tm, k), lambda i, j: (i, 0)),
                  pl.BlockSpec((k, tn), lambda i, j: (0, j), **w_mode),
                  pl.BlockSpec((tm, tn), lambda i, j: (i, j))],
        out_specs=pl.BlockSpec((tm, tn), lambda i, j: (i, j)),
        out_shape=jax.ShapeDtypeStruct((m, n), F32),
        compiler_params=_cparams(2),
        name="matmul_res",
    )(a, w, res)


def _gdn_kernel(q_ref, k_ref, v_ref, z_ref, sm_ref, cwq_ref, cwk_ref, cwv_ref, alog_ref, dtb_ref,
                gn_ref, o_ref,
                s_ref, pq_ref, pk_ref, pv_ref, qs_ref, ks_ref, vs_ref,
                u_ref, w_ref, qd_ref, kt_ref, a_ref, gl_ref, *, heads):
    head0 = pl.program_id(1) * heads
    tb = q_ref.shape[0]
    n_chunks = tb // CHUNK
    hd = HEAD_DIM

    @pl.when(pl.program_id(2) == 0)
    def _():
        s_ref[...] = jnp.zeros_like(s_ref)
        pq_ref[...] = jnp.zeros_like(pq_ref)
        pk_ref[...] = jnp.zeros_like(pk_ref)
        pv_ref[...] = jnp.zeros_like(pv_ref)

    def conv_silu(x_ref, p_ref, cw_ref):
        x = x_ref[...]
        y = _silu(_causal_conv(x, p_ref[...], cw_ref, GDN_CONV))
        p_ref[...] = x[tb - 8:tb, :]
        return y

    q = conv_silu(q_ref, pq_ref, cwq_ref)
    k = conv_silu(k_ref, pk_ref, cwk_ref)
    vs_ref[...] = conv_silu(v_ref, pv_ref, cwv_ref)
    for hh in range(heads):
        cs = slice(hh * hd, (hh + 1) * hd)
        qh = q[:, cs]
        kh = k[:, cs]
        qs_ref[:, cs] = qh * lax.rsqrt(jnp.sum(qh * qh, axis=-1, keepdims=True) + EPS) * (hd ** -0.5)
        ks_ref[:, cs] = kh * lax.rsqrt(jnp.sum(kh * kh, axis=-1, keepdims=True) + EPS)

    r, c = _tri_masks(CHUNK)
    incl = c <= r
    strict = c < r
    tril01 = jnp.where(incl, 1.0, 0.0).astype(BF16)
    eye = jnp.where(r == c, 1.0, 0.0).astype(F32)
    lane = lax.broadcasted_iota(jnp.int32, (1, SMALL_W), 1)
    oh_a = [jnp.where(lane == LANE_A + head0 + hh, 1.0, 0.0).astype(F32) for hh in range(heads)]
    oh_b = [jnp.where(lane == LANE_BETA + head0 + hh, 1.0, 0.0).astype(F32) for hh in range(heads)]
    neg_a = -jnp.exp(alog_ref[...])
    dtb = dtb_ref[...]

    def prep(it, carry):
        rows, rows8, cum, beta_full = [], [], [], []
        for cj in range(GDN_PREP_CHUNKS):
            ci = it * GDN_PREP_CHUNKS + cj
            rows.append(pl.ds(pl.multiple_of(ci * CHUNK, CHUNK), CHUNK))
            rows8.append(pl.ds(pl.multiple_of(ci * 8, 8), 8))
            sm = sm_ref[rows[cj], :]
            g_full = neg_a * _softplus(sm + dtb)
            cum.append(_dot_exact_a(tril01, g_full))
            beta_full.append(_sigmoid(sm))
        units = [(cj, hh) for cj in range(GDN_PREP_CHUNKS) for hh in range(heads)]
        us = range(len(units))
        cs = [slice(hh * hd, (hh + 1) * hd) for _, hh in units]
        qc = [qs_ref[rows[cj], cs[i]] for i, (cj, hh) in enumerate(units)]
        kc = [ks_ref[rows[cj], cs[i]] for i, (cj, hh) in enumerate(units)]
        vc = [vs_ref[rows[cj], cs[i]] for i, (cj, hh) in enumerate(units)]
        g_col = [jnp.sum(cum[cj] * oh_a[hh], axis=-1, keepdims=True) for cj, hh in units]
        oh8 = [jnp.broadcast_to(oh_a[hh], (8, SMALL_W)).astype(BF16) for hh in range(heads)]
        g_row = [_dot_exact_a(oh8[hh], cum[cj], _NT)[0:1, :] for cj, hh in units]
        beta = [jnp.sum(beta_full[cj] * oh_b[hh], axis=-1, keepdims=True) for cj, hh in units]
        decay = [jnp.exp(jnp.where(incl, g_col[i] - g_row[i], -jnp.inf)) for i in us]
        kb = [kc[i] * beta[i] for i in us]
        kk = [_dot(kb[i], kc[i], _NT) for i in us]
        qk = [_dot(qc[i], kc[i], _NT) for i in us]
        p = [-jnp.where(strict, kk[i] * decay[i], 0.0) for i in us]
        t = [eye + p[i] for i in us]
        for _ in range(5):
            p = _dot_hi_many(p, p)
            t = [t[i] + d for i, d in zip(us, _dot_hi_many(t, p))]
        eg = [jnp.exp(g_col[i]) for i in us]
        g_last = [g_col[i][CHUNK - 1:CHUNK, :] for i in us]
        u = [_dot(t[i], vc[i] * beta[i]) for i in us]
        w = [_dot(t[i], kb[i] * eg[i]) for i in us]
        for i, (cj, hh) in enumerate(units):
            u_ref[rows[cj], cs[i]] = u[i]
            w_ref[rows[cj], cs[i]] = w[i]
            a_ref[hh, rows[cj], :] = qk[i] * decay[i]
            qd_ref[rows[cj], cs[i]] = qc[i] * eg[i]
            kt_ref[rows[cj], cs[i]] = kc[i] * jnp.exp(g_last[i] - g_col[i])
            gl_ref[hh, rows8[cj], :] = jnp.broadcast_to(jnp.exp(g_last[i]), (8, hd))
        return carry

    lax.fori_loop(0, n_chunks // GDN_PREP_CHUNKS, prep, 0)

    gn = gn_ref[...]

    def scan(ci, carry):
        rows = pl.ds(pl.multiple_of(ci * CHUNK, CHUNK), CHUNK)
        hs = range(heads)
        cs = [slice(hh * hd, (hh + 1) * hd) for hh in hs]
        s = [s_ref[hh] for hh in hs]
        sb = [s[hh].astype(BF16) for hh in hs]
        ws = [_dot(w_ref[rows, cs[hh]], sb[hh]) for hh in hs]
        qs = [_dot(qd_ref[rows, cs[hh]], sb[hh]) for hh in hs]
        vb = [(u_ref[rows, cs[hh]] - ws[hh]).astype(BF16) for hh in hs]
        av = [_dot(a_ref[hh, rows, :], vb[hh]) for hh in hs]
        kv = [_dot(kt_ref[rows, cs[hh]], vb[hh], _TN) for hh in hs]
        for hh in hs:
            gl = gl_ref[hh, pl.ds(pl.multiple_of(ci * 8, 8), 1), :]
            s_ref[hh] = s[hh] * gl + kv[hh]
            y = _rms_rows(qs[hh] + av[hh], gn) * _silu(z_ref[rows, cs[hh]])
            o_ref[rows, cs[hh]] = y.astype(o_ref.dtype)
        return carry

    lax.fori_loop(0, n_chunks, scan, 0)


def gdn_mixer(proj, conv_w, alog_vec, dtb_vec, gnorm, *, tb, heads):
    b, s, _ = proj.shape
    assert s % tb == 0 and tb % CHUNK == 0 and GDN_HEADS % heads == 0
    hd = HEAD_DIM
    gw = heads * hd
    assert all(off % gw == 0 for off in (OFF_QKV_A, GDN_W, OFF_Z_A))
    col = lambda off: (lambda bi, gi, ti: (bi, ti, off // gw + gi))
    cwcol = lambda off: (lambda bi, gi, ti: (0, off // gw + gi))
    const = lambda bi, gi, ti: (0, 0)
    blk = (None, tb, gw)
    return pl.pallas_call(
        functools.partial(_gdn_kernel, heads=heads),
        grid=(b, GDN_HEADS // heads, s // tb),
        in_specs=[pl.BlockSpec(blk, col(OFF_QKV_A)),
                  pl.BlockSpec(blk, col(OFF_QKV_A + GDN_W)),
                  pl.BlockSpec(blk, col(OFF_QKV_A + 2 * GDN_W)),
                  pl.BlockSpec(blk, col(OFF_Z_A)),
                  pl.BlockSpec((None, tb, SMALL_W), lambda bi, gi, ti: (bi, ti, OFF_SMALL // SMALL_W)),
                  pl.BlockSpec((GDN_CONV, gw), cwcol(0)),
                  pl.BlockSpec((GDN_CONV, gw), cwcol(GDN_W)),
                  pl.BlockSpec((GDN_CONV, gw), cwcol(2 * GDN_W)),
                  pl.BlockSpec((1, SMALL_W), const),
                  pl.BlockSpec((1, SMALL_W), const),
                  pl.BlockSpec((1, hd), const)],
        out_specs=pl.BlockSpec(blk, lambda bi, gi, ti: (bi, ti, gi)),
        out_shape=jax.ShapeDtypeStruct((b, s, GDN_W), BF16),
        scratch_shapes=[pltpu.VMEM((heads, hd, hd), F32)]
                       + [pltpu.VMEM((8, gw), F32)] * 3
                       + [pltpu.VMEM((tb, gw), F32)] * 3
                       + [pltpu.VMEM((tb, gw), F32)] * 4
                       + [pltpu.VMEM((heads, tb, CHUNK), F32),
                          pltpu.VMEM((heads, tb // CHUNK * 8, hd), F32)],
        compiler_params=_cparams(3),
        name="gdn_mixer",
    )(proj, proj, proj, proj, proj, conv_w, conv_w, conv_w, alog_vec, dtb_vec, gnorm.reshape(1, hd))


def _mlstm_kernel(q_ref, k_ref, v_ref, og_ref, sm_ref, ib_ref, fb_ref, gn_ref, o_ref,
                  c_ref, n_ref, m_ref, *, heads):
    head0 = pl.program_id(1) * heads
    tb = q_ref.shape[0]
    n_chunks = tb // CHUNK
    hd = HEAD_DIM

    @pl.when(pl.program_id(2) == 0)
    def _():
        c_ref[...] = jnp.zeros_like(c_ref)
        n_ref[...] = jnp.zeros_like(n_ref)
        m_ref[...] = jnp.zeros_like(m_ref)

    r, c = _tri_masks(CHUNK)
    incl = c <= r
    tril01 = jnp.where(incl, 1.0, 0.0).astype(BF16)
    lane = lax.broadcasted_iota(jnp.int32, (1, SMALL_W), 1)
    oh_i = [jnp.where(lane == LANE_I + head0 + hh, 1.0, 0.0).astype(F32) for hh in range(heads)]
    oh_f = [jnp.where(lane == LANE_F + head0 + hh, 1.0, 0.0).astype(F32) for hh in range(heads)]
    ib = ib_ref[...]
    fb = fb_ref[...]
    gn = gn_ref[...]

    def chunk(ci, carry):
        rows = pl.ds(pl.multiple_of(ci * CHUNK, CHUNK), CHUNK)
        sm = sm_ref[rows, :]
        i_full = sm + ib
        b_full = _dot_exact_a(tril01, _log_sigmoid(sm + fb))
        hs = range(heads)
        cs = [slice(hh * hd, (hh + 1) * hd) for hh in hs]
        qc = [q_ref[rows, cs[hh]] * (hd ** -0.5) for hh in hs]
        kc = [k_ref[rows, cs[hh]] for hh in hs]
        vc = [v_ref[rows, cs[hh]] for hh in hs]
        cmat = [c_ref[hh] for hh in hs]
        nvec = [n_ref[hh] for hh in hs]
        m = [m_ref[hh, 0:1, 0:1] for hh in hs]
        s_qk = [_dot(qc[hh], kc[hh], _NT) for hh in hs]
        q_c = [_dot(qc[hh], cmat[hh]) for hh in hs]
        b_col = [jnp.sum(b_full * oh_f[hh], axis=-1, keepdims=True) for hh in hs]
        i_col = [jnp.sum(i_full * oh_i[hh], axis=-1, keepdims=True) for hh in hs]
        oh_f8 = [jnp.broadcast_to(oh_f[hh], (8, SMALL_W)).astype(BF16) for hh in hs]
        oh_i8 = [jnp.broadcast_to(oh_i[hh], (8, SMALL_W)).astype(BF16) for hh in hs]
        b_row = [_dot_exact_a(oh_f8[hh], b_full, _NT)[0:1, :] for hh in hs]
        i_row = [_dot_exact_a(oh_i8[hh], i_full, _NT)[0:1, :] for hh in hs]
        dmat = [jnp.where(incl, b_col[hh] - b_row[hh] + i_row[hh], -jnp.inf) for hh in hs]
        d_max = [jnp.max(dmat[hh], axis=-1, keepdims=True) for hh in hs]
        b_last = [b_col[hh][CHUNK - 1:CHUNK, :] for hh in hs]
        a_end = [b_last[hh] - b_col[hh] + i_col[hh] for hh in hs]
        a_max = [jnp.max(a_end[hh], axis=0, keepdims=True) for hh in hs]
        inter = [b_col[hh] + m[hh] for hh in hs]
        m_t = [jnp.maximum(inter[hh], d_max[hh]) for hh in hs]
        w_inter = [jnp.exp(inter[hh] - m_t[hh]) for hh in hs]
        wmat = [jnp.exp(dmat[hh] - m_t[hh]) * s_qk[hh] for hh in hs]
        m_new = [jnp.maximum(b_last[hh] + m[hh], a_max[hh]) for hh in hs]
        w_old = [jnp.exp(b_last[hh] + m[hh] - m_new[hh]) for hh in hs]
        kw = [kc[hh] * jnp.exp(a_end[hh] - m_new[hh]) for hh in hs]
        w_v = [_dot(wmat[hh], vc[hh]) for hh in hs]
        k_v = [_dot(kw[hh], vc[hh], _TN) for hh in hs]
        for hh in hs:
            num = w_inter[hh] * q_c[hh] + w_v[hh]
            den = (w_inter[hh] * jnp.sum(qc[hh] * nvec[hh], axis=-1, keepdims=True)
                   + jnp.sum(wmat[hh], axis=-1, keepdims=True))
            hout = num / jnp.maximum(jnp.abs(den), jnp.exp(-m_t[hh]))
            c_ref[hh] = w_old[hh] * cmat[hh] + k_v[hh]
            n_ref[hh] = w_old[hh] * nvec[hh] + jnp.sum(kw[hh], axis=0, keepdims=True)
            m_ref[hh] = jnp.broadcast_to(m_new[hh], m_ref.shape[1:])
            y = _rms_rows(hout, gn) * _sigmoid(og_ref[rows, cs[hh]])
            o_ref[rows, cs[hh]] = y.astype(o_ref.dtype)
        return carry

    lax.fori_loop(0, n_chunks, chunk, 0)


def mlstm_mixer(proj, ib_vec, fb_vec, gnorm, *, tb, heads):
    b, s, _ = proj.shape
    assert s % tb == 0 and tb % CHUNK == 0 and ML_HEADS % heads == 0
    hd = HEAD_DIM
    gw = heads * hd
    assert all(off % gw == 0 for off in (OFF_QB, OFF_KB, OFF_VB, OFF_OB))
    col = lambda off: (lambda bi, gi, ti: (bi, ti, off // gw + gi))
    const = lambda bi, gi, ti: (0, 0)
    blk = (None, tb, gw)
    return pl.pallas_call(
        functools.partial(_mlstm_kernel, heads=heads),
        grid=(b, ML_HEADS // heads, s // tb),
        in_specs=[pl.BlockSpec(blk, col(OFF_QB)),
                  pl.BlockSpec(blk, col(OFF_KB)),
                  pl.BlockSpec(blk, col(OFF_VB)),
                  pl.BlockSpec(blk, col(OFF_OB)),
                  pl.BlockSpec((None, tb, SMALL_W), lambda bi, gi, ti: (bi, ti, OFF_SMALL // SMALL_W)),
                  pl.BlockSpec((1, SMALL_W), const),
                  pl.BlockSpec((1, SMALL_W), const),
                  pl.BlockSpec((1, hd), const)],
        out_specs=pl.BlockSpec(blk, lambda bi, gi, ti: (bi, ti, gi)),
        out_shape=jax.ShapeDtypeStruct((b, s, ML_W), BF16),
        scratch_shapes=[pltpu.VMEM((heads, hd, hd), F32),
                        pltpu.VMEM((heads, 1, hd), F32),
                        pltpu.VMEM((heads, 8, hd), F32)],
        compiler_params=_cparams(3),
        name="mlstm_mixer",
    )(proj, proj, proj, proj, proj, ib_vec, fb_vec, gnorm.reshape(1, hd))


def _sb_kernel(q_ref, k_ref, v_ref, o_ref, acc_ref, carry_ref, *, heads):
    tq = q_ref.shape[0]
    hd = HEAD_DIM
    qi = pl.program_id(2)
    hs = range(heads)
    cs = [slice(hh * hd, (hh + 1) * hd) for hh in hs]
    qb = [(q_ref[:, cs[hh]] * (hd ** -0.5)).astype(BF16) for hh in hs]
    r, c = _tri_masks(tq)
    strict = c < r
    later01 = jnp.where(r > c, 1.0, 0.0).astype(BF16)

    acc_ref[...] = jnp.zeros_like(acc_ref)
    carry_ref[...] = jnp.zeros_like(carry_ref)

    def process(kb, diag):
        rows = pl.ds(pl.multiple_of(kb * tq, tq), tq)
        d = functools.partial(lax.dot_general, preferred_element_type=F32)
        z = [d(qb[hh], k_ref[rows, cs[hh]].astype(BF16), _NT) for hh in hs]
        sp = [jnp.log1p(jnp.exp(-jnp.abs(z[hh]))) for hh in hs]
        log_beta = [jnp.minimum(z[hh], 0.0) - sp[hh] for hh in hs]
        log_stay = [jnp.minimum(-z[hh], 0.0) - sp[hh] for hh in hs]
        if diag:
            log_stay = [jnp.where(strict, ls, 0.0) for ls in log_stay]
        parts = [_split_bf16(ls, 2) for ls in log_stay]
        in_hi = [d(parts[hh][0], later01, _NN) for hh in hs]
        in_lo = [d(parts[hh][1], later01, _NN) for hh in hs]
        inblk = [in_hi[hh] + in_lo[hh] for hh in hs]
        carry = [carry_ref[hh] for hh in hs]
        a = [jnp.exp(log_beta[hh] + inblk[hh] + carry[hh]) for hh in hs]
        if diag:
            a = [jnp.where(strict, x, 0.0) for x in a]
        av = [_dot(a[hh], v_ref[rows, cs[hh]]) for hh in hs]
        top = None
        for hh in hs:
            acc_ref[:, cs[hh]] += av[hh]
            new_carry = carry[hh] + inblk[hh][:, 0:1] + log_stay[hh][:, 0:1]
            carry_ref[hh] = new_carry
            mx = jnp.max(new_carry)
            top = mx if top is None else jnp.maximum(top, mx)
        return top

    top = process(qi, True)

    def cond(st):
        return jnp.logical_and(st[0] >= 0, st[1] >= SB_DEAD_LOG)

    def body(st):
        return st[0] - 1, process(st[0], False)

    lax.while_loop(cond, body, (qi - 1, top))
    o_ref[...] = acc_ref[...].astype(o_ref.dtype)


def sb_mixer(proj, *, tq, heads):
    b, s, _ = proj.shape
    assert s % tq == 0 and SB_HEADS % heads == 0
    gw = heads * HEAD_DIM
    assert all(off % gw == 0 for off in (OFF_QC, OFF_KC, OFF_VC))
    resident = dict(pipeline_mode=pl.Buffered(1))
    return pl.pallas_call(
        functools.partial(_sb_kernel, heads=heads),
        grid=(b, SB_HEADS // heads, s // tq),
        in_specs=[pl.BlockSpec((None, tq, gw), lambda bi, gi, ti: (bi, ti, OFF_QC // gw + gi)),
                  pl.BlockSpec((None, s, gw), lambda bi, gi, ti: (bi, 0, OFF_KC // gw + gi), **resident),
                  pl.BlockSpec((None, s, gw), lambda bi, gi, ti: (bi, 0, OFF_VC // gw + gi), **resident)],
        out_specs=pl.BlockSpec((None, tq, gw), lambda bi, gi, ti: (bi, ti, gi)),
        out_shape=jax.ShapeDtypeStruct((b, s, SB_W), BF16),
        scratch_shapes=[pltpu.VMEM((tq, gw), F32), pltpu.VMEM((heads, tq, 1), F32)],
        compiler_params=_cparams(3),
        name="sb_mixer",
    )(proj, proj, proj)


def _merge_kernel(ya_ref, yb_ref, yc_ref, wa_ref, wb_ref, wc_ref, ga_ref, gb_ref, gc_ref, o_ref):
    d = functools.partial(jnp.dot, preferred_element_type=F32)
    m = (_sigmoid(ga_ref[...]) * d(ya_ref[...], wa_ref[...])
         + _sigmoid(gb_ref[...]) * d(yb_ref[...], wb_ref[...])
         + _sigmoid(gc_ref[...]) * d(yc_ref[...], wc_ref[...]))
    o_ref[...] = m.astype(o_ref.dtype)


def merge_branches(ya, yb, yc, wa, wb, wc, proj2d, *, tm, tn):
    m = ya.shape[0]
    n = wa.shape[1]
    assert m % tm == 0 and n % tn == 0
    gate = lambda which: (lambda i, j: (i, (OFF_GATE + which * D_MODEL) // tn + j))
    row = lambda i, j: (i, 0)
    wcol = lambda i, j: (0, j)
    return pl.pallas_call(
        _merge_kernel,
        grid=(m // tm, n // tn),
        in_specs=[pl.BlockSpec((tm, ya.shape[1]), row),
                  pl.BlockSpec((tm, yb.shape[1]), row),
                  pl.BlockSpec((tm, yc.shape[1]), row),
                  pl.BlockSpec((wa.shape[0], tn), wcol),
                  pl.BlockSpec((wb.shape[0], tn), wcol),
                  pl.BlockSpec((wc.shape[0], tn), wcol),
                  pl.BlockSpec((tm, tn), gate(0)),
                  pl.BlockSpec((tm, tn), gate(1)),
                  pl.BlockSpec((tm, tn), gate(2))],
        out_specs=pl.BlockSpec((tm, tn), lambda i, j: (i, j)),
        out_shape=jax.ShapeDtypeStruct((m, n), BF16),
        compiler_params=_cparams(2),
        name="merge_branches",
    )(ya, yb, yc, wa, wb, wc, proj2d, proj2d, proj2d)


def _xattn_kernel(q_ref, kv_ref, gq_ref, gk_ref, o_ref):
    gq = gq_ref[...]
    gk = gk_ref[...]
    for hh in range(XA_HEADS):
        cs = slice(hh * XA_DH, (hh + 1) * XA_DH)
        qh = _rms_rows(q_ref[:, cs], gq)
        kh = _rms_rows(kv_ref[:, cs], gk)
        vh = kv_ref[:, D_MODEL + hh * XA_DH:D_MODEL + (hh + 1) * XA_DH]
        logits = _dot(qh, kh, _NT) * (XA_DH ** -0.5)
        mx = jnp.max(logits, axis=-1, keepdims=True)
        e = jnp.exp(logits - mx)
        p = e / jnp.sum(e, axis=-1, keepdims=True)
        o_ref[:, cs] = _dot(p, vh).astype(o_ref.dtype)


def cross_attention(q, kv, gq, gk, *, tm):
    b, s, d = q.shape
    mlen = kv.shape[1]
    assert s % tm == 0
    const = lambda bi, ti: (0, 0)
    return pl.pallas_call(
        _xattn_kernel,
        grid=(b, s // tm),
        in_specs=[pl.BlockSpec((None, tm, d), lambda bi, ti: (bi, ti, 0)),
                  pl.BlockSpec((None, mlen, 2 * d), lambda bi, ti: (bi, 0, 0)),
                  pl.BlockSpec((1, XA_DH), const),
                  pl.BlockSpec((1, XA_DH), const)],
        out_specs=pl.BlockSpec((None, tm, d), lambda bi, ti: (bi, ti, 0)),
        out_shape=jax.ShapeDtypeStruct((b, s, d), BF16),
        compiler_params=_cparams(2),
        name="cross_attention",
    )(q, kv, gq.reshape(1, XA_DH), gk.reshape(1, XA_DH))


def _ffn_up_kernel(x_ref, gn_ref, wg_ref, wu_ref, cg_ref, cu_ref, o_ref,
                   xn_ref, pg_ref, pu_ref, *, blocks_per_seq):
    i = pl.program_id(0)
    j = pl.program_id(1)
    tm = x_ref.shape[0]

    @pl.when(j == 0)
    def _():
        xn_ref[...] = _rms_rows(x_ref[...], gn_ref[...]).astype(BF16)

    @pl.when(i % blocks_per_seq == 0)
    def _():
        pg_ref[j] = jnp.zeros(pg_ref.shape[1:], F32)
        pu_ref[j] = jnp.zeros(pu_ref.shape[1:], F32)

    xn = xn_ref[...]
    g = jnp.dot(xn, wg_ref[...], preferred_element_type=F32)
    u = jnp.dot(xn, wu_ref[...], preferred_element_type=F32)
    gc = _causal_conv(g, pg_ref[j], cg_ref, FFN_CONV)
    uc = _causal_conv(u, pu_ref[j], cu_ref, FFN_CONV)
    pg_ref[j] = g[tm - 8:tm, :]
    pu_ref[j] = u[tm - 8:tm, :]
    o_ref[...] = (_silu(gc) * uc).astype(o_ref.dtype)


def ffn_up_glu(x, gn, w_up, w_conv, *, tm, tn, seq_len):
    m, k = x.shape
    assert m % tm == 0 and D_FF % tn == 0 and seq_len % tm == 0
    nj = D_FF // tn
    kern = functools.partial(_ffn_up_kernel, blocks_per_seq=seq_len // tm)
    return pl.pallas_call(
        kern,
        grid=(m // tm, nj),
        in_specs=[pl.BlockSpec((tm, k), lambda i, j: (i, 0)),
                  pl.BlockSpec((1, k), lambda i, j: (0, 0)),
                  pl.BlockSpec((k, tn), lambda i, j: (0, j)),
                  pl.BlockSpec((k, tn), lambda i, j: (0, nj + j)),
                  pl.BlockSpec((FFN_CONV, tn), lambda i, j: (0, j)),
                  pl.BlockSpec((FFN_CONV, tn), lambda i, j: (0, nj + j))],
        out_specs=pl.BlockSpec((tm, tn), lambda i, j: (i, j)),
        out_shape=jax.ShapeDtypeStruct((m, D_FF), BF16),
        scratch_shapes=[pltpu.VMEM((tm, k), BF16),
                        pltpu.VMEM((nj, 8, tn), F32),
                        pltpu.VMEM((nj, 8, tn), F32)],
        compiler_params=_cparams(2),
        name="ffn_up",
    )(x, gn.reshape(1, k), w_up, w_up, w_conv, w_conv)


def _pad_lanes(v, lane0, width):
    out = jnp.zeros((1, width), F32)
    return lax.dynamic_update_slice(out, v.reshape(1, -1).astype(F32), (0, lane0))


def _arrange_w_in_t(w):
    sizes = (3 * GDN_W, GDN_HEADS, GDN_HEADS, GDN_W, ML_W, ML_W, ML_W, ML_HEADS, ML_HEADS, ML_W,
             SB_W, SB_W, SB_W, 3 * D_MODEL)
    offs = [0]
    for sz in sizes:
        offs.append(offs[-1] + sz)
    wt = w.T
    part = lambda i: wt[offs[i]:offs[i + 1]]
    (qkv_a, a_pre, b_pre, z_a, q_b, k_b, v_b, i_b, f_b, o_b, q_c, k_c, v_c, gate) = (
        part(i) for i in range(len(sizes)))
    small = jnp.concatenate([a_pre, b_pre, i_b, f_b], axis=0)
    small = jnp.pad(small, ((0, SMALL_W - small.shape[0]), (0, 0)))
    return jnp.concatenate([gate, qkv_a, z_a, q_b, k_b, v_b, o_b, q_c, k_c, v_c, small],
                           axis=0).astype(BF16)


def _tiles(seq):
    return dict(
        in_proj=dict(tm=min(1024, seq), tn=768),
        square=dict(tm=min(512, seq), tn=D_MODEL),
        mem_kv=dict(tm=512, tn=1024),
        ffn_up=dict(tm=min(1024, seq), tn=512),
        ffn_down=dict(tm=min(256, seq), tn=D_MODEL),
        merge=dict(tm=min(256, seq), tn=D_MODEL),
        xattn=dict(tm=min(512, seq)),
        gdn=dict(tb=min(1024, seq), heads=3),
        mlstm=dict(tb=min(1024, seq), heads=4),
        sb=dict(tq=min(256, seq), heads=2),
    )


def _layer(x, mem2d, p, *, batch, seq, mem_len):
    n = batch * seq
    t = _tiles(seq)
    proj = rms_matmul(x, p["norm_mix"], p["w_in_t"], w_transposed=True, **t["in_proj"])
    proj3 = proj.reshape(batch, seq, PROJ_W)
    ya = gdn_mixer(proj3, p["gdn_conv"], p["alog_vec"], p["dtb_vec"], p["gdn_norm"], **t["gdn"])
    yb = mlstm_mixer(proj3, p["ib_vec"], p["fb_vec"], p["ml_norm"], **t["mlstm"])
    yc = sb_mixer(proj3, **t["sb"])
    merged = merge_branches(ya.reshape(n, GDN_W), yb.reshape(n, ML_W), yc.reshape(n, SB_W),
                            p["wb_a"], p["wb_b"], p["wb_c"], proj, **t["merge"])
    x = matmul_res(merged, p["w_out"], x, **t["square"])
    q = rms_matmul(x, p["norm_xa"], p["xa_wq"], **t["square"])
    kv = rms_matmul(mem2d, p["norm_mem"], p["xa_wkv"], **t["mem_kv"])
    o = cross_attention(q.reshape(batch, seq, D_MODEL), kv.reshape(batch, mem_len, 2 * D_MODEL),
                        p["xa_qnorm"], p["xa_knorm"], **t["xattn"])
    x = matmul_res(o.reshape(n, D_MODEL), p["xa_wo"], x, **t["square"])
    act = ffn_up_glu(x, p["norm_ffn"], p["ffn_up"], p["ffn_conv"], seq_len=seq, **t["ffn_up"])
    x = matmul_res(act, p["ffn_down"], x, **t["ffn_down"])
    return x


def kernel(x, mem, norm_mix, w_in, gdn_conv, gdn_a_log, gdn_dt_bias, gdn_norm, ml_gate_bias, ml_norm, w_br, w_out, norm_xa, norm_mem, xa_wq, xa_wkv, xa_wo, xa_qnorm, xa_knorm, norm_ffn, ffn_up, ffn_conv, ffn_down):
    batch, seq, d = x.shape
    mem_len = mem.shape[1]
    depth = w_in.shape[0]
    h = x.reshape(batch * seq, d)
    mem2d = mem.reshape(batch * mem_len, d)
    for l in range(depth):
        p = {
            "norm_mix": norm_mix[l],
            "w_in_t": _arrange_w_in_t(w_in[l]),
            "gdn_conv": gdn_conv[l],
            "alog_vec": _pad_lanes(gdn_a_log[l], LANE_A, SMALL_W),
            "dtb_vec": _pad_lanes(gdn_dt_bias[l], LANE_A, SMALL_W),
            "gdn_norm": gdn_norm[l],
            "ib_vec": _pad_lanes(ml_gate_bias[l, 0], LANE_I, SMALL_W),
            "fb_vec": _pad_lanes(ml_gate_bias[l, 1], LANE_F, SMALL_W),
            "ml_norm": ml_norm[l],
            "wb_a": w_br[l, :GDN_W].astype(BF16),
            "wb_b": w_br[l, GDN_W:GDN_W + ML_W].astype(BF16),
            "wb_c": w_br[l, GDN_W + ML_W:].astype(BF16),
            "w_out": w_out[l].astype(BF16),
            "norm_xa": norm_xa[l],
            "norm_mem": norm_mem[l],
            "xa_wq": xa_wq[l].astype(BF16),
            "xa_wkv": xa_wkv[l].astype(BF16),
            "xa_wo": xa_wo[l].astype(BF16),
            "xa_qnorm": xa_qnorm[l],
            "xa_knorm": xa_knorm[l],
            "norm_ffn": norm_ffn[l],
            "ffn_up": ffn_up[l].astype(BF16),
            "ffn_conv": ffn_conv[l],
            "ffn_down": ffn_down[l].astype(BF16),
        }
        h = _layer(h, mem2d, p, batch=batch, seq=seq, mem_len=mem_len)
    return h.reshape(batch, seq, d)
```

```python
import functools

import jax
import jax.numpy as jnp
from jax import lax
from jax.experimental import pallas as pl
from jax.experimental.pallas import tpu as pltpu

F32 = jnp.float32
BF16 = jnp.bfloat16

EPS = 1e-6
CHUNK = 64
D_MODEL = 2048
HEAD_DIM = 128
GDN_HEADS = 6
ML_HEADS = 4
SB_HEADS = 6
GDN_CONV = 4
GDN_PREP_CHUNKS = 4
XA_HEADS = 4
XA_DH = D_MODEL // XA_HEADS
D_FF = 5632
FFN_CONV = 3

GDN_W = GDN_HEADS * HEAD_DIM
ML_W = ML_HEADS * HEAD_DIM
SB_W = SB_HEADS * HEAD_DIM

OFF_GATE = 0
OFF_QKV_A = OFF_GATE + 3 * D_MODEL
OFF_Z_A = OFF_QKV_A + 3 * GDN_W
OFF_QB = OFF_Z_A + GDN_W
OFF_KB = OFF_QB + ML_W
OFF_VB = OFF_KB + ML_W
OFF_OB = OFF_VB + ML_W
OFF_QC = OFF_OB + ML_W
OFF_KC = OFF_QC + SB_W
OFF_VC = OFF_KC + SB_W
OFF_SMALL = OFF_VC + SB_W
SMALL_W = 256
PROJ_W = OFF_SMALL + SMALL_W
LANE_A = 0
LANE_BETA = LANE_A + GDN_HEADS
LANE_I = LANE_BETA + GDN_HEADS
LANE_F = LANE_I + ML_HEADS

SB_DEAD_LOG = -105.0

VMEM_LIMIT = 56 * 1024 * 1024


def _cparams(n_axes):
    return pltpu.CompilerParams(dimension_semantics=("arbitrary",) * n_axes,
                                vmem_limit_bytes=VMEM_LIMIT)


def _rms_rows(x, g):
    return x * lax.rsqrt(jnp.mean(x * x, axis=-1, keepdims=True) + EPS) * g


def _sigmoid(x):
    return 1.0 / (1.0 + jnp.exp(-x))


def _silu(x):
    return x * _sigmoid(x)


def _softplus(x):
    return jnp.maximum(x, 0.0) + jnp.log1p(jnp.exp(-jnp.abs(x)))


def _log_sigmoid(x):
    return -_softplus(-x)


def _split_bf16(x, parts):
    out = []
    r = x
    for i in range(parts):
        p = r.astype(BF16)
        out.append(p)
        if i + 1 < parts:
            r = r - p.astype(F32)
    return out


_NN = (((1,), (0,)), ((), ()))
_NT = (((1,), (1,)), ((), ()))
_TN = (((0,), (0,)), ((), ()))


def _dot(a, b, dims=_NN):
    return lax.dot_general(a.astype(BF16), b.astype(BF16), dims, preferred_element_type=F32)


def _dot_exact_b(a, b01, dims=_NN):
    acc = None
    for p in _split_bf16(a, 3):
        t = lax.dot_general(p, b01, dims, preferred_element_type=F32)
        acc = t if acc is None else acc + t
    return acc


def _dot_exact_a(a01, b, dims=_NN):
    acc = None
    for p in _split_bf16(b, 3):
        t = lax.dot_general(a01, p, dims, preferred_element_type=F32)
        acc = t if acc is None else acc + t
    return acc


def _dot_hi_many(a_list, b_list):
    sa = [_split_bf16(a, 2) for a in a_list]
    sb = [_split_bf16(b, 2) for b in b_list]
    d = functools.partial(lax.dot_general, dimension_numbers=_NN, preferred_element_type=F32)
    hh = [d(a[0], b[0]) for a, b in zip(sa, sb)]
    hl = [d(a[0], b[1]) for a, b in zip(sa, sb)]
    lh = [d(a[1], b[0]) for a, b in zip(sa, sb)]
    return [x + (y + z) for x, y, z in zip(hh, hl, lh)]


def _causal_conv(x, prev8, w_ref, taps):
    def taps_sum(y):
        out = y * w_ref[taps - 1:taps, :]
        for d in range(1, taps):
            out = out + pltpu.roll(y, d, axis=0) * w_ref[taps - 1 - d:taps - d, :]
        return out

    head = taps_sum(jnp.concatenate([prev8, x[0:8, :]], axis=0))[8:16, :]
    return jnp.concatenate([head, taps_sum(x)[8:, :]], axis=0)


def _tri_masks(n):
    r = lax.broadcasted_iota(jnp.int32, (n, n), 0)
    c = lax.broadcasted_iota(jnp.int32, (n, n), 1)
    return r, c


def _rms_matmul_kernel(x_ref, g_ref, w_ref, o_ref, xn_ref, *, w_dims):
    @pl.when(pl.program_id(1) == 0)
    def _():
        xn_ref[...] = _rms_rows(x_ref[...], g_ref[...]).astype(BF16)

    o_ref[...] = lax.dot_general(xn_ref[...], w_ref[...], w_dims,
                                 preferred_element_type=F32).astype(o_ref.dtype)


def rms_matmul(x, g, w, *, tm, tn, w_transposed=False):
    m, k = x.shape
    n = w.shape[0] if w_transposed else w.shape[1]
    assert m % tm == 0 and n % tn == 0
    w_mode = dict(pipeline_mode=pl.Buffered(1)) if tn == n else {}
    if w_transposed:
        w_spec = pl.BlockSpec((tn, k), lambda i, j: (j, 0), **w_mode)
    else:
        w_spec = pl.BlockSpec((k, tn), lambda i, j: (0, j), **w_mode)
    return pl.pallas_call(
        functools.partial(_rms_matmul_kernel, w_dims=_NT if w_transposed else _NN),
        grid=(m // tm, n // tn),
        in_specs=[pl.BlockSpec((tm, k), lambda i, j: (i, 0)),
                  pl.BlockSpec((1, k), lambda i, j: (0, 0)),
                  w_spec],
        out_specs=pl.BlockSpec((tm, tn), lambda i, j: (i, j)),
        out_shape=jax.ShapeDtypeStruct((m, n), F32),
        scratch_shapes=[pltpu.VMEM((tm, k), BF16)],
        compiler_params=_cparams(2),
        name="rms_matmul",
    )(x, g.reshape(1, k), w)


def _matmul_res_kernel(a_ref, w_ref, r_ref, o_ref):
    o_ref[...] = r_ref[...] + jnp.dot(a_ref[...], w_ref[...], preferred_element_type=F32)


def matmul_res(a, w, res, *, tm, tn):
    m, k = a.shape
    n = w.shape[1]
    assert m % tm == 0 and n % tn == 0
    w_mode = dict(pipeline_mode=pl.Buffered(1)) if tn == n else {}
    return pl.pallas_call(
        _matmul_res_kernel,
        grid=(m // tm, n // tn),
        in_specs=[pl.BlockSpec((tm, k), lambda i, j: (i, 0)),
                  pl.BlockSpec((k, tn), lambda i, j: (0, j), **w_mode),
                  pl.BlockSpec((tm, tn), lambda i, j: (i, j))],
        out_specs=pl.BlockSpec((tm, tn), lambda i, j: (i, j)),
        out_shape=jax.ShapeDtypeStruct((m, n), F32),
        compiler_params=_cparams(2),
        name="matmul_res",
    )(a, w, res)


def _gdn_kernel(q_ref, k_ref, v_ref, z_ref, sm_ref, cwq_ref, cwk_ref, cwv_ref, alog_ref, dtb_ref,
                gn_ref, o_ref,
                s_ref, pq_ref, pk_ref, pv_ref, qs_ref, ks_ref, vs_ref,
                u_ref, w_ref, qd_ref, kt_ref, a_ref, gl_ref, *, heads):
    head0 = pl.program_id(1) * heads
    tb = q_ref.shape[0]
    n_chunks = tb // CHUNK
    hd = HEAD_DIM

    @pl.when(pl.program_id(2) == 0)
    def _():
        s_ref[...] = jnp.zeros_like(s_ref)
        pq_ref[...] = jnp.zeros_like(pq_ref)
        pk_ref[...] = jnp.zeros_like(pk_ref)
        pv_ref[...] = jnp.zeros_like(pv_ref)

    def conv_silu(x_ref, p_ref, cw_ref):
        x = x_ref[...]
        y = _silu(_causal_conv(x, p_ref[...], cw_ref, GDN_CONV))
        p_ref[...] = x[tb - 8:tb, :]
        return y

    q = conv_silu(q_ref, pq_ref, cwq_ref)
    k = conv_silu(k_ref, pk_ref, cwk_ref)
    vs_ref[...] = conv_silu(v_ref, pv_ref, cwv_ref)
    for hh in range(heads):
        cs = slice(hh * hd, (hh + 1) * hd)
        qh = q[:, cs]
        kh = k[:, cs]
        qs_ref[:, cs] = qh * lax.rsqrt(jnp.sum(qh * qh, axis=-1, keepdims=True) + EPS) * (hd ** -0.5)
        ks_ref[:, cs] = kh * lax.rsqrt(jnp.sum(kh * kh, axis=-1, keepdims=True) + EPS)

    r, c = _tri_masks(CHUNK)
    incl = c <= r
    strict = c < r
    tril01 = jnp.where(incl, 1.0, 0.0).astype(BF16)
    eye = jnp.where(r == c, 1.0, 0.0).astype(F32)
    lane = lax.broadcasted_iota(jnp.int32, (1, SMALL_W), 1)
    oh_a = [jnp.where(lane == LANE_A + head0 + hh, 1.0, 0.0).astype(F32) for hh in range(heads)]
    oh_b = [jnp.where(lane == LANE_BETA + head0 + hh, 1.0, 0.0).astype(F32) for hh in range(heads)]
    neg_a = -jnp.exp(alog_ref[...])
    dtb = dtb_ref[...]

    def prep(it, carry):
        rows, rows8, cum, beta_full = [], [], [], []
        for cj in range(GDN_PREP_CHUNKS):
            ci = it * GDN_PREP_CHUNKS + cj
            rows.append(pl.ds(pl.multiple_of(ci * CHUNK, CHUNK), CHUNK))
            rows8.append(pl.ds(pl.multiple_of(ci * 8, 8), 8))
            sm = sm_ref[rows[cj], :]
            g_full = neg_a * _softplus(sm + dtb)
            cum.append(_dot_exact_a(tril01, g_full))
            beta_full.append(_sigmoid(sm))
        units = [(cj, hh) for cj in range(GDN_PREP_CHUNKS) for hh in range(heads)]
        us = range(len(units))
        cs = [slice(hh * hd, (hh + 1) * hd) for _, hh in units]
        qc = [qs_ref[rows[cj], cs[i]] for i, (cj, hh) in enumerate(units)]
        kc = [ks_ref[rows[cj], cs[i]] for i, (cj, hh) in enumerate(units)]
        vc = [vs_ref[rows[cj], cs[i]] for i, (cj, hh) in enumerate(units)]
        g_col = [jnp.sum(cum[cj] * oh_a[hh], axis=-1, keepdims=True) for cj, hh in units]
        oh8 = [jnp.broadcast_to(oh_a[hh], (8, SMALL_W)).astype(BF16) for hh in range(heads)]
        g_row = [_dot_exact_a(oh8[hh], cum[cj], _NT)[0:1, :] for cj, hh in units]
        beta = [jnp.sum(beta_full[cj] * oh_b[hh], axis=-1, keepdims=True) for cj, hh in units]
        decay = [jnp.exp(jnp.where(incl, g_col[i] - g_row[i], -jnp.inf)) for i in us]
        kb = [kc[i] * beta[i] for i in us]
        kk = [_dot(kb[i], kc[i], _NT) for i in us]
        qk = [_dot(qc[i], kc[i], _NT) for i in us]
        p = [-jnp.where(strict, kk[i] * decay[i], 0.0) for i in us]
        t = [eye + p[i] for i in us]
        for _ in range(5):
            p = _dot_hi_many(p, p)
            t = [t[i] + d for i, d in zip(us, _dot_hi_many(t, p))]
        eg = [jnp.exp(g_col[i]) for i in us]
        g_last = [g_col[i][CHUNK - 1:CHUNK, :] for i in us]
        u = [_dot(t[i], vc[i] * beta[i]) for i in us]
        w = [_dot(t[i], kb[i] * eg[i]) for i in us]
        for i, (cj, hh) in enumerate(units):
            u_ref[rows[cj], cs[i]] = u[i]
            w_ref[rows[cj], cs[i]] = w[i]
            a_ref[hh, rows[cj], :] = qk[i] * decay[i]
            qd_ref[rows[cj], cs[i]] = qc[i] * eg[i]
            kt_ref[rows[cj], cs[i]] = kc[i] * jnp.exp(g_last[i] - g_col[i])
            gl_ref[hh, rows8[cj], :] = jnp.broadcast_to(jnp.exp(g_last[i]), (8, hd))
        return carry

    lax.fori_loop(0, n_chunks // GDN_PREP_CHUNKS, prep, 0)

    gn = gn_ref[...]

    def scan(ci, carry):
        rows = pl.ds(pl.multiple_of(ci * CHUNK, CHUNK), CHUNK)
        hs = range(heads)
        cs = [slice(hh * hd, (hh + 1) * hd) for hh in hs]
        s = [s_ref[hh] for hh in hs]
        sb = [s[hh].astype(BF16) for hh in hs]
        ws = [_dot(w_ref[rows, cs[hh]], sb[hh]) for hh in hs]
        qs = [_dot(qd_ref[rows, cs[hh]], sb[hh]) for hh in hs]
        vb = [(u_ref[rows, cs[hh]] - ws[hh]).astype(BF16) for hh in hs]
        av = [_dot(a_ref[hh, rows, :], vb[hh]) for hh in hs]
        kv = [_dot(kt_ref[rows, cs[hh]], vb[hh], _TN) for hh in hs]
        for hh in hs:
            gl = gl_ref[hh, pl.ds(pl.multiple_of(ci * 8, 8), 1), :]
            s_ref[hh] = s[hh] * gl + kv[hh]
            y = _rms_rows(qs[hh] + av[hh], gn) * _silu(z_ref[rows, cs[hh]])
            o_ref[rows, cs[hh]] = y.astype(o_ref.dtype)
        return carry

    lax.fori_loop(0, n_chunks, scan, 0)


def gdn_mixer(proj, conv_w, alog_vec, dtb_vec, gnorm, *, tb, heads):
    b, s, _ = proj.shape
    assert s % tb == 0 and tb % CHUNK == 0 and GDN_HEADS % heads == 0
    hd = HEAD_DIM
    gw = heads * hd
    assert all(off % gw == 0 for off in (OFF_QKV_A, GDN_W, OFF_Z_A))
    col = lambda off: (lambda bi, gi, ti: (bi, ti, off // gw + gi))
    cwcol = lambda off: (lambda bi, gi, ti: (0, off // gw + gi))
    const = lambda bi, gi, ti: (0, 0)
    blk = (None, tb, gw)
    return pl.pallas_call(
        functools.partial(_gdn_kernel, heads=heads),
        grid=(b, GDN_HEADS // heads, s // tb),
        in_specs=[pl.BlockSpec(blk, col(OFF_QKV_A)),
                  pl.BlockSpec(blk, col(OFF_QKV_A + GDN_W)),
                  pl.BlockSpec(blk, col(OFF_QKV_A + 2 * GDN_W)),
                  pl.BlockSpec(blk, col(OFF_Z_A)),
                  pl.BlockSpec((None, tb, SMALL_W), lambda bi, gi, ti: (bi, ti, OFF_SMALL // SMALL_W)),
                  pl.BlockSpec((GDN_CONV, gw), cwcol(0)),
                  pl.BlockSpec((GDN_CONV, gw), cwcol(GDN_W)),
                  pl.BlockSpec((GDN_CONV, gw), cwcol(2 * GDN_W)),
                  pl.BlockSpec((1, SMALL_W), const),
                  pl.BlockSpec((1, SMALL_W), const),
                  pl.BlockSpec((1, hd), const)],
        out_specs=pl.BlockSpec(blk, lambda bi, gi, ti: (bi, ti, gi)),
        out_shape=jax.ShapeDtypeStruct((b, s, GDN_W), BF16),
        scratch_shapes=[pltpu.VMEM((heads, hd, hd), F32)]
                       + [pltpu.VMEM((8, gw), F32)] * 3
                       + [pltpu.VMEM((tb, gw), F32)] * 3
                       + [pltpu.VMEM((tb, gw), F32)] * 4
                       + [pltpu.VMEM((heads, tb, CHUNK), F32),
                          pltpu.VMEM((heads, tb // CHUNK * 8, hd), F32)],
        compiler_params=_cparams(3),
        name="gdn_mixer",
    )(proj, proj, proj, proj, proj, conv_w, conv_w, conv_w, alog_vec, dtb_vec, gnorm.reshape(1, hd))


def _mlstm_kernel(q_ref, k_ref, v_ref, og_ref, sm_ref, ib_ref, fb_ref, gn_ref, o_ref,
                  c_ref, n_ref, m_ref, *, heads):
    head0 = pl.program_id(1) * heads
    tb = q_ref.shape[0]
    n_chunks = tb // CHUNK
    hd = HEAD_DIM

    @pl.when(pl.program_id(2) == 0)
    def _():
        c_ref[...] = jnp.zeros_like(c_ref)
        n_ref[...] = jnp.zeros_like(n_ref)
        m_ref[...] = jnp.zeros_like(m_ref)

    r, c = _tri_masks(CHUNK)
    incl = c <= r
    tril01 = jnp.where(incl, 1.0, 0.0).astype(BF16)
    lane = lax.broadcasted_iota(jnp.int32, (1, SMALL_W), 1)
    oh_i = [jnp.where(lane == LANE_I + head0 + hh, 1.0, 0.0).astype(F32) for hh in range(heads)]
    oh_f = [jnp.where(lane == LANE_F + head0 + hh, 1.0, 0.0).astype(F32) for hh in range(heads)]
    ib = ib_ref[...]
    fb = fb_ref[...]
    gn = gn_ref[...]

    def chunk(ci, carry):
        rows = pl.ds(pl.multiple_of(ci * CHUNK, CHUNK), CHUNK)
        sm = sm_ref[rows, :]
        i_full = sm + ib
        b_full = _dot_exact_a(tril01, _log_sigmoid(sm + fb))
        hs = range(heads)
        cs = [slice(hh * hd, (hh + 1) * hd) for hh in hs]
        qc = [q_ref[rows, cs[hh]] * (hd ** -0.5) for hh in hs]
        kc = [k_ref[rows, cs[hh]] for hh in hs]
        vc = [v_ref[rows, cs[hh]] for hh in hs]
        cmat = [c_ref[hh] for hh in hs]
        nvec = [n_ref[hh] for hh in hs]
        m = [m_ref[hh, 0:1, 0:1] for hh in hs]
        s_qk = [_dot(qc[hh], kc[hh], _NT) for hh in hs]
        q_c = [_dot(qc[hh], cmat[hh]) for hh in hs]
        b_col = [jnp.sum(b_full * oh_f[hh], axis=-1, keepdims=True) for hh in hs]
        i_col = [jnp.sum(i_full * oh_i[hh], axis=-1, keepdims=True) for hh in hs]
        oh_f8 = [jnp.broadcast_to(oh_f[hh], (8, SMALL_W)).astype(BF16) for hh in hs]
        oh_i8 = [jnp.broadcast_to(oh_i[hh], (8, SMALL_W)).astype(BF16) for hh in hs]
        b_row = [_dot_exact_a(oh_f8[hh], b_full, _NT)[0:1, :] for hh in hs]
        i_row = [_dot_exact_a(oh_i8[hh], i_full, _NT)[0:1, :] for hh in hs]
        dmat = [jnp.where(incl, b_col[hh] - b_row[hh] + i_row[hh], -jnp.inf) for hh in hs]
        d_max = [jnp.max(dmat[hh], axis=-1, keepdims=True) for hh in hs]
        b_last = [b_col[hh][CHUNK - 1:CHUNK, :] for hh in hs]
        a_end = [b_last[hh] - b_col[hh] + i_col[hh] for hh in hs]
        a_max = [jnp.max(a_end[hh], axis=0, keepdims=True) for hh in hs]
        inter = [b_col[hh] + m[hh] for hh in hs]
        m_t = [jnp.maximum(inter[hh], d_max[hh]) for hh in hs]
        w_inter = [jnp.exp(inter[hh] - m_t[hh]) for hh in hs]
        wmat = [jnp.exp(dmat[hh] - m_t[hh]) * s_qk[hh] for hh in hs]
        m_new = [jnp.maximum(b_last[hh] + m[hh], a_max[hh]) for hh in hs]
        w_old = [jnp.exp(b_last[hh] + m[hh] - m_new[hh]) for hh in hs]
        kw = [kc[hh] * jnp.exp(a_end[hh] - m_new[hh]) for hh in hs]
        w_v = [_dot(wmat[hh], vc[hh]) for hh in hs]
        k_v = [_dot(kw[hh], vc[hh], _TN) for hh in hs]
        for hh in hs:
            num = w_inter[hh] * q_c[hh] + w_v[hh]
            den = (w_inter[hh] * jnp.sum(qc[hh] * nvec[hh], axis=-1, keepdims=True)
                   + jnp.sum(wmat[hh], axis=-1, keepdims=True))
            hout = num / jnp.maximum(jnp.abs(den), jnp.exp(-m_t[hh]))
            c_ref[hh] = w_old[hh] * cmat[hh] + k_v[hh]
            n_ref[hh] = w_old[hh] * nvec[hh] + jnp.sum(kw[hh], axis=0, keepdims=True)
            m_ref[hh] = jnp.broadcast_to(m_new[hh], m_ref.shape[1:])
            y = _rms_rows(hout, gn) * _sigmoid(og_ref[rows, cs[hh]])
            o_ref[rows, cs[hh]] = y.astype(o_ref.dtype)
        return carry

    lax.fori_loop(0, n_chunks, chunk, 0)


def mlstm_mixer(proj, ib_vec, fb_vec, gnorm, *, tb, heads):
    b, s, _ = proj.shape
    assert s % tb == 0 and tb % CHUNK == 0 and ML_HEADS % heads == 0
    hd = HEAD_DIM
    gw = heads * hd
    assert all(off % gw == 0 for off in (OFF_QB, OFF_KB, OFF_VB, OFF_OB))
    col = lambda off: (lambda bi, gi, ti: (bi, ti, off // gw + gi))
    const = lambda bi, gi, ti: (0, 0)
    blk = (None, tb, gw)
    return pl.pallas_call(
        functools.partial(_mlstm_kernel, heads=heads),
        grid=(b, ML_HEADS // heads, s // tb),
        in_specs=[pl.BlockSpec(blk, col(OFF_QB)),
                  pl.BlockSpec(blk, col(OFF_KB)),
                  pl.BlockSpec(blk, col(OFF_VB)),
                  pl.BlockSpec(blk, col(OFF_OB)),
                  pl.BlockSpec((None, tb, SMALL_W), lambda bi, gi, ti: (bi, ti, OFF_SMALL // SMALL_W)),
                  pl.BlockSpec((1, SMALL_W), const),
                  pl.BlockSpec((1, SMALL_W), const),
                  pl.BlockSpec((1, hd), const)],
        out_specs=pl.BlockSpec(blk, lambda bi, gi, ti: (bi, ti, gi)),
        out_shape=jax.ShapeDtypeStruct((b, s, ML_W), BF16),
        scratch_shapes=[pltpu.VMEM((heads, hd, hd), F32),
                        pltpu.VMEM((heads, 1, hd), F32),
                        pltpu.VMEM((heads, 8, hd), F32)],
        compiler_params=_cparams(3),
        name="mlstm_mixer",
    )(proj, proj, proj, proj, proj, ib_vec, fb_vec, gnorm.reshape(1, hd))


def _sb_kernel(q_ref, k_ref, v_ref, o_ref, acc_ref, carry_ref, *, heads):
    tq = q_ref.shape[0]
    hd = HEAD_DIM
    qi = pl.program_id(2)
    hs = range(heads)
    cs = [slice(hh * hd, (hh + 1) * hd) for hh in hs]
    qb = [(q_ref[:, cs[hh]] * (hd ** -0.5)).astype(BF16) for hh in hs]
    r, c = _tri_masks(tq)
    strict = c < r
    later01 = jnp.where(r > c, 1.0, 0.0).astype(BF16)

    acc_ref[...] = jnp.zeros_like(acc_ref)
    carry_ref[...] = jnp.zeros_like(carry_ref)

    def process(kb, diag):
        rows = pl.ds(pl.multiple_of(kb * tq, tq), tq)
        d = functools.partial(lax.dot_general, preferred_element_type=F32)
        z = [d(qb[hh], k_ref[rows, cs[hh]].astype(BF16), _NT) for hh in hs]
        sp = [jnp.log(1.0 + jnp.exp(-jnp.abs(z[hh]))) for hh in hs]
        log_beta = [jnp.minimum(z[hh], 0.0) - sp[hh] for hh in hs]
        log_stay = [jnp.minimum(-z[hh], 0.0) - sp[hh] for hh in hs]
        if diag:
            log_stay = [jnp.where(strict, ls, 0.0) for ls in log_stay]
        parts = [_split_bf16(ls, 2) for ls in log_stay]
        in_hi = [d(parts[hh][0], later01, _NN) for hh in hs]
        in_lo = [d(parts[hh][1], later01, _NN) for hh in hs]
        inblk = [in_hi[hh] + in_lo[hh] for hh in hs]
        carry = [carry_ref[hh] for hh in hs]
        a = [jnp.exp(log_beta[hh] + inblk[hh] + carry[hh]) for hh in hs]
        if diag:
            a = [jnp.where(strict, x, 0.0) for x in a]
        av = [_dot(a[hh], v_ref[rows, cs[hh]]) for hh in hs]
        top = None
        for hh in hs:
            acc_ref[:, cs[hh]] += av[hh]
            new_carry = carry[hh] + inblk[hh][:, 0:1] + log_stay[hh][:, 0:1]
            carry_ref[hh] = new_carry
            mx = jnp.max(new_carry)
            top = mx if top is None else jnp.maximum(top, mx)
        return top

    top = process(qi, True)

    def cond(st):
        return jnp.logical_and(st[0] >= 0, st[1] >= SB_DEAD_LOG)

    def body(st):
        return st[0] - 1, process(st[0], False)

    lax.while_loop(cond, body, (qi - 1, top))
    o_ref[...] = acc_ref[...].astype(o_ref.dtype)


def sb_mixer(proj, *, tq, heads):
    b, s, _ = proj.shape
    assert s % tq == 0 and SB_HEADS % heads == 0
    gw = heads * HEAD_DIM
    assert all(off % gw == 0 for off in (OFF_QC, OFF_KC, OFF_VC))
    resident = dict(pipeline_mode=pl.Buffered(1))
    return pl.pallas_call(
        functools.partial(_sb_kernel, heads=heads),
        grid=(b, SB_HEADS // heads, s // tq),
        in_specs=[pl.BlockSpec((None, tq, gw), lambda bi, gi, ti: (bi, ti, OFF_QC // gw + gi)),
                  pl.BlockSpec((None, s, gw), lambda bi, gi, ti: (bi, 0, OFF_KC // gw + gi), **resident),
                  pl.BlockSpec((None, s, gw), lambda bi, gi, ti: (bi, 0, OFF_VC // gw + gi), **resident)],
        out_specs=pl.BlockSpec((None, tq, gw), lambda bi, gi, ti: (bi, ti, gi)),
        out_shape=jax.ShapeDtypeStruct((b, s, SB_W), BF16),
        scratch_shapes=[pltpu.VMEM((tq, gw), F32), pltpu.VMEM((heads, tq, 1), F32)],
        compiler_params=_cparams(3),
        name="sb_mixer",
    )(proj, proj, proj)


def _merge_kernel(ya_ref, yb_ref, yc_ref, wa_ref, wb_ref, wc_ref, ga_ref, gb_ref, gc_ref, o_ref):
    d = functools.partial(jnp.dot, preferred_element_type=F32)
    m = (_sigmoid(ga_ref[...]) * d(ya_ref[...], wa_ref[...])
         + _sigmoid(gb_ref[...]) * d(yb_ref[...], wb_ref[...])
         + _sigmoid(gc_ref[...]) * d(yc_ref[...], wc_ref[...]))
    o_ref[...] = m.astype(o_ref.dtype)


def merge_branches(ya, yb, yc, wa, wb, wc, proj2d, *, tm, tn):
    m = ya.shape[0]
    n = wa.shape[1]
    assert m % tm == 0 and n % tn == 0
    gate = lambda which: (lambda i, j: (i, (OFF_GATE + which * D_MODEL) // tn + j))
    row = lambda i, j: (i, 0)
    wcol = lambda i, j: (0, j)
    return pl.pallas_call(
        _merge_kernel,
        grid=(m // tm, n // tn),
        in_specs=[pl.BlockSpec((tm, ya.shape[1]), row),
                  pl.BlockSpec((tm, yb.shape[1]), row),
                  pl.BlockSpec((tm, yc.shape[1]), row),
                  pl.BlockSpec((wa.shape[0], tn), wcol),
                  pl.BlockSpec((wb.shape[0], tn), wcol),
                  pl.BlockSpec((wc.shape[0], tn), wcol),
                  pl.BlockSpec((tm, tn), gate(0)),
                  pl.BlockSpec((tm, tn), gate(1)),
                  pl.BlockSpec((tm, tn), gate(2))],
        out_specs=pl.BlockSpec((tm, tn), lambda i, j: (i, j)),
        out_shape=jax.ShapeDtypeStruct((m, n), BF16),
        compiler_params=_cparams(2),
        name="merge_branches",
    )(ya, yb, yc, wa, wb, wc, proj2d, proj2d, proj2d)


def _xattn_kernel(q_ref, kv_ref, gq_ref, gk_ref, o_ref):
    gq = gq_ref[...]
    gk = gk_ref[...]
    for hh in range(XA_HEADS):
        cs = slice(hh * XA_DH, (hh + 1) * XA_DH)
        qh = _rms_rows(q_ref[:, cs], gq)
        kh = _rms_rows(kv_ref[:, cs], gk)
        vh = kv_ref[:, D_MODEL + hh * XA_DH:D_MODEL + (hh + 1) * XA_DH]
        logits = _dot(qh, kh, _NT) * (XA_DH ** -0.5)
        mx = jnp.max(logits, axis=-1, keepdims=True)
        e = jnp.exp(logits - mx)
        p = e / jnp.sum(e, axis=-1, keepdims=True)
        o_ref[:, cs] = _dot(p, vh).astype(o_ref.dtype)


def cross_attention(q, kv, gq, gk, *, tm):
    b, s, d = q.shape
    mlen = kv.shape[1]
    assert s % tm == 0
    const = lambda bi, ti: (0, 0)
    return pl.pallas_call(
        _xattn_kernel,
        grid=(b, s // tm),
        in_specs=[pl.BlockSpec((None, tm, d), lambda bi, ti: (bi, ti, 0)),
                  pl.BlockSpec((None, mlen, 2 * d), lambda bi, ti: (bi, 0, 0)),
                  pl.BlockSpec((1, XA_DH), const),
                  pl.BlockSpec((1, XA_DH), const)],
        out_specs=pl.BlockSpec((None, tm, d), lambda bi, ti: (bi, ti, 0)),
        out_shape=jax.ShapeDtypeStruct((b, s, d), BF16),
        compiler_params=_cparams(2),
        name="cross_attention",
    )(q, kv, gq.reshape(1, XA_DH), gk.reshape(1, XA_DH))


def _ffn_up_kernel(x_ref, gn_ref, wg_ref, wu_ref, cg_ref, cu_ref, o_ref,
                   xn_ref, pg_ref, pu_ref, *, blocks_per_seq):
    i = pl.program_id(0)
    j = pl.program_id(1)
    tm = x_ref.shape[0]

    @pl.when(j == 0)
    def _():
        xn_ref[...] = _rms_rows(x_ref[...], gn_ref[...]).astype(BF16)

    @pl.when(i % blocks_per_seq == 0)
    def _():
        pg_ref[j] = jnp.zeros(pg_ref.shape[1:], F32)
        pu_ref[j] = jnp.zeros(pu_ref.shape[1:], F32)

    xn = xn_ref[...]
    g = jnp.dot(xn, wg_ref[...], preferred_element_type=F32)
    u = jnp.dot(xn, wu_ref[...], preferred_element_type=F32)
    gc = _causal_conv(g, pg_ref[j], cg_ref, FFN_CONV)
    uc = _causal_conv(u, pu_ref[j], cu_ref, FFN_CONV)
    pg_ref[j] = g[tm - 8:tm, :]
    pu_ref[j] = u[tm - 8:tm, :]
    o_ref[...] = (_silu(gc) * uc).astype(o_ref.dtype)


def ffn_up_glu(x, gn, w_up, w_conv, *, tm, tn, seq_len):
    m, k = x.shape
    assert m % tm == 0 and D_FF % tn == 0 and seq_len % tm == 0
    nj = D_FF // tn
    kern = functools.partial(_ffn_up_kernel, blocks_per_seq=seq_len // tm)
    return pl.pallas_call(
        kern,
        grid=(m // tm, nj),
        in_specs=[pl.BlockSpec((tm, k), lambda i, j: (i, 0)),
                  pl.BlockSpec((1, k), lambda i, j: (0, 0)),
                  pl.BlockSpec((k, tn), lambda i, j: (0, j)),
                  pl.BlockSpec((k, tn), lambda i, j: (0, nj + j)),
                  pl.BlockSpec((FFN_CONV, tn), lambda i, j: (0, j)),
                  pl.BlockSpec((FFN_CONV, tn), lambda i, j: (0, nj + j))],
        out_specs=pl.BlockSpec((tm, tn), lambda i, j: (i, j)),
        out_shape=jax.ShapeDtypeStruct((m, D_FF), BF16),
        scratch_shapes=[pltpu.VMEM((tm, k), BF16),
                        pltpu.VMEM((nj, 8, tn), F32),
                        pltpu.VMEM((nj, 8, tn), F32)],
        compiler_params=_cparams(2),
        name="ffn_up",
    )(x, gn.reshape(1, k), w_up, w_up, w_conv, w_conv)


def _pad_lanes(v, lane0, width):
    out = jnp.zeros((1, width), F32)
    return lax.dynamic_update_slice(out, v.reshape(1, -1).astype(F32), (0, lane0))


def _arrange_w_in_t(w):
    sizes = (3 * GDN_W, GDN_HEADS, GDN_HEADS, GDN_W, ML_W, ML_W, ML_W, ML_HEADS, ML_HEADS, ML_W,
             SB_W, SB_W, SB_W, 3 * D_MODEL)
    offs = [0]
    for sz in sizes:
        offs.append(offs[-1] + sz)
    wt = w.T
    part = lambda i: wt[offs[i]:offs[i + 1]]
    (qkv_a, a_pre, b_pre, z_a, q_b, k_b, v_b, i_b, f_b, o_b, q_c, k_c, v_c, gate) = (
        part(i) for i in range(len(sizes)))
    small = jnp.concatenate([a_pre, b_pre, i_b, f_b], axis=0)
    small = jnp.pad(small, ((0, SMALL_W - small.shape[0]), (0, 0)))
    return jnp.concatenate([gate, qkv_a, z_a, q_b, k_b, v_b, o_b, q_c, k_c, v_c, small],
                           axis=0).astype(BF16)


def _tiles(seq):
    return dict(
        in_proj=dict(tm=min(1024, seq), tn=1536),
        square=dict(tm=min(512, seq), tn=D_MODEL),
        mem_kv=dict(tm=512, tn=1024),
        ffn_up=dict(tm=min(1024, seq), tn=512),
        ffn_down=dict(tm=min(256, seq), tn=D_MODEL),
        merge=dict(tm=min(256, seq), tn=D_MODEL),
        xattn=dict(tm=min(512, seq)),
        gdn=dict(tb=min(1024, seq), heads=3),
        mlstm=dict(tb=min(1024, seq), heads=4),
        sb=dict(tq=min(256, seq), heads=2),
    )


def _layer(x, mem2d, p, *, batch, seq, mem_len):
    n = batch * seq
    t = _tiles(seq)
    proj = rms_matmul(x, p["norm_mix"], p["w_in_t"], w_transposed=True, **t["in_proj"])
    proj3 = proj.reshape(batch, seq, PROJ_W)
    ya = gdn_mixer(proj3, p["gdn_conv"], p["alog_vec"], p["dtb_vec"], p["gdn_norm"], **t["gdn"])
    yb = mlstm_mixer(proj3, p["ib_vec"], p["fb_vec"], p["ml_norm"], **t["mlstm"])
    yc = sb_mixer(proj3, **t["sb"])
    merged = merge_branches(ya.reshape(n, GDN_W), yb.reshape(n, ML_W), yc.reshape(n, SB_W),
                            p["wb_a"], p["wb_b"], p["wb_c"], proj, **t["merge"])
    x = matmul_res(merged, p["w_out"], x, **t["square"])
    q = rms_matmul(x, p["norm_xa"], p["xa_wq"], **t["square"])
    kv = rms_matmul(mem2d, p["norm_mem"], p["xa_wkv"], **t["mem_kv"])
    o = cross_attention(q.reshape(batch, seq, D_MODEL), kv.reshape(batch, mem_len, 2 * D_MODEL),
                        p["xa_qnorm"], p["xa_knorm"], **t["xattn"])
    x = matmul_res(o.reshape(n, D_MODEL), p["xa_wo"], x, **t["square"])
    act = ffn_up_glu(x, p["norm_ffn"], p["ffn_up"], p["ffn_conv"], seq_len=seq, **t["ffn_up"])
    x = matmul_res(act, p["ffn_down"], x, **t["ffn_down"])
    return x


def kernel(x, mem, norm_mix, w_in, gdn_conv, gdn_a_log, gdn_dt_bias, gdn_norm, ml_gate_bias, ml_norm, w_br, w_out, norm_xa, norm_mem, xa_wq, xa_wkv, xa_wo, xa_qnorm, xa_knorm, norm_ffn, ffn_up, ffn_conv, ffn_down):
    batch, seq, d = x.shape
    mem_len = mem.shape[1]
    depth = w_in.shape[0]
    h = x.reshape(batch * seq, d)
    mem2d = mem.reshape(batch * mem_len, d)
    for l in range(depth):
        p = {
            "norm_mix": norm_mix[l],
            "w_in_t": _arrange_w_in_t(w_in[l]),
            "gdn_conv": gdn_conv[l],
            "alog_vec": _pad_lanes(gdn_a_log[l], LANE_A, SMALL_W),
            "dtb_vec": _pad_lanes(gdn_dt_bias[l], LANE_A, SMALL_W),
            "gdn_norm": gdn_norm[l],
            "ib_vec": _pad_lanes(ml_gate_bias[l, 0], LANE_I, SMALL_W),
            "fb_vec": _pad_lanes(ml_gate_bias[l, 1], LANE_F, SMALL_W),
            "ml_norm": ml_norm[l],
            "wb_a": w_br[l, :GDN_W].astype(BF16),
            "wb_b": w_br[l, GDN_W:GDN_W + ML_W].astype(BF16),
            "wb_c": w_br[l, GDN_W + ML_W:].astype(BF16),
            "w_out": w_out[l].astype(BF16),
            "norm_xa": norm_xa[l],
            "norm_mem": norm_mem[l],
            "xa_wq": xa_wq[l].astype(BF16),
            "xa_wkv": xa_wkv[l].astype(BF16),
            "xa_wo": xa_wo[l].astype(BF16),
            "xa_qnorm": xa_qnorm[l],
            "xa_knorm": xa_knorm[l],
            "norm_ffn": norm_ffn[l],
            "ffn_up": ffn_up[l].astype(BF16),
            "ffn_conv": ffn_conv[l],
            "ffn_down": ffn_down[l].astype(BF16),
        }
        h = _layer(h, mem2d, p, batch=batch, seq=seq, mem_len=mem_len)
    return h.reshape(batch, seq, d)
```

```python
import functools

import jax
import jax.numpy as jnp
from jax import lax
from jax.experimental import pallas as pl
from jax.experimental.pallas import tpu as pltpu

F32 = jnp.float32
BF16 = jnp.bfloat16

EPS = 1e-6
CHUNK = 64
D_MODEL = 2048
HEAD_DIM = 128
GDN_HEADS = 6
ML_HEADS = 4
SB_HEADS = 6
GDN_CONV = 4
GDN_PREP_CHUNKS = 4
ML_CHUNK_GROUP = 4
XA_HEADS = 4
XA_DH = D_MODEL // XA_HEADS
D_FF = 5632
FFN_CONV = 3

GDN_W = GDN_HEADS * HEAD_DIM
ML_W = ML_HEADS * HEAD_DIM
SB_W = SB_HEADS * HEAD_DIM

OFF_GATE = 0
OFF_QKV_A = OFF_GATE + 3 * D_MODEL
OFF_Z_A = OFF_QKV_A + 3 * GDN_W
OFF_QB = OFF_Z_A + GDN_W
OFF_KB = OFF_QB + ML_W
OFF_VB = OFF_KB + ML_W
OFF_OB = OFF_VB + ML_W
OFF_QC = OFF_OB + ML_W
OFF_KC = OFF_QC + SB_W
OFF_VC = OFF_KC + SB_W
OFF_SMALL = OFF_VC + SB_W
SMALL_W = 256
PROJ_W = OFF_SMALL + SMALL_W
LANE_A = 0
LANE_BETA = LANE_A + GDN_HEADS
LANE_I = LANE_BETA + GDN_HEADS
LANE_F = LANE_I + ML_HEADS

SB_DEAD_LOG = -105.0

VMEM_LIMIT = 56 * 1024 * 1024


def _cparams(n_axes):
    return pltpu.CompilerParams(dimension_semantics=("arbitrary",) * n_axes,
                                vmem_limit_bytes=VMEM_LIMIT)


def _rms_rows(x, g):
    return x * lax.rsqrt(jnp.mean(x * x, axis=-1, keepdims=True) + EPS) * g


def _sigmoid(x):
    return 1.0 / (1.0 + jnp.exp(-x))


def _silu(x):
    return x * _sigmoid(x)


def _softplus(x):
    return jnp.maximum(x, 0.0) + jnp.log1p(jnp.exp(-jnp.abs(x)))


def _log_sigmoid(x):
    return -_softplus(-x)


def _split_bf16(x, parts):
    out = []
    r = x
    for i in range(parts):
        p = r.astype(BF16)
        out.append(p)
        if i + 1 < parts:
            r = r - p.astype(F32)
    return out


_NN = (((1,), (0,)), ((), ()))
_NT = (((1,), (1,)), ((), ()))
_TN = (((0,), (0,)), ((), ()))


def _dot(a, b, dims=_NN):
    return lax.dot_general(a.astype(BF16), b.astype(BF16), dims, preferred_element_type=F32)


def _dot_exact_b(a, b01, dims=_NN):
    acc = None
    for p in _split_bf16(a, 3):
        t = lax.dot_general(p, b01, dims, preferred_element_type=F32)
        acc = t if acc is None else acc + t
    return acc


def _dot_exact_a(a01, b, dims=_NN):
    acc = None
    for p in _split_bf16(b, 3):
        t = lax.dot_general(a01, p, dims, preferred_element_type=F32)
        acc = t if acc is None else acc + t
    return acc


def _dot_hi_many(a_list, b_list):
    sa = [_split_bf16(a, 2) for a in a_list]
    sb = [_split_bf16(b, 2) for b in b_list]
    d = functools.partial(lax.dot_general, dimension_numbers=_NN, preferred_element_type=F32)
    hh = [d(a[0], b[0]) for a, b in zip(sa, sb)]
    hl = [d(a[0], b[1]) for a, b in zip(sa, sb)]
    lh = [d(a[1], b[0]) for a, b in zip(sa, sb)]
    return [x + (y + z) for x, y, z in zip(hh, hl, lh)]


def _causal_conv(x, prev8, w_ref, taps):
    def taps_sum(y):
        out = y * w_ref[taps - 1:taps, :]
        for d in range(1, taps):
            out = out + pltpu.roll(y, d, axis=0) * w_ref[taps - 1 - d:taps - d, :]
        return out

    head = taps_sum(jnp.concatenate([prev8, x[0:8, :]], axis=0))[8:16, :]
    return jnp.concatenate([head, taps_sum(x)[8:, :]], axis=0)


def _tri_masks(n):
    r = lax.broadcasted_iota(jnp.int32, (n, n), 0)
    c = lax.broadcasted_iota(jnp.int32, (n, n), 1)
    return r, c


def _rms_matmul_kernel(x_ref, g_ref, w_ref, o_ref, xn_ref, *, w_dims):
    @pl.when(pl.program_id(1) == 0)
    def _():
        xn_ref[...] = _rms_rows(x_ref[...], g_ref[...]).astype(BF16)

    o_ref[...] = lax.dot_general(xn_ref[...], w_ref[...], w_dims,
                                 preferred_element_type=F32).astype(o_ref.dtype)


def rms_matmul(x, g, w, *, tm, tn, w_transposed=False):
    m, k = x.shape
    n = w.shape[0] if w_transposed else w.shape[1]
    assert m % tm == 0 and n % tn == 0
    w_mode = dict(pipeline_mode=pl.Buffered(1)) if tn == n else {}
    if w_transposed:
        w_spec = pl.BlockSpec((tn, k), lambda i, j: (j, 0), **w_mode)
    else:
        w_spec = pl.BlockSpec((k, tn), lambda i, j: (0, j), **w_mode)
    return pl.pallas_call(
        functools.partial(_rms_matmul_kernel, w_dims=_NT if w_transposed else _NN),
        grid=(m // tm, n // tn),
        in_specs=[pl.BlockSpec((tm, k), lambda i, j: (i, 0)),
                  pl.BlockSpec((1, k), lambda i, j: (0, 0)),
                  w_spec],
        out_specs=pl.BlockSpec((tm, tn), lambda i, j: (i, j)),
        out_shape=jax.ShapeDtypeStruct((m, n), F32),
        scratch_shapes=[pltpu.VMEM((tm, k), BF16)],
        compiler_params=_cparams(2),
        name="rms_matmul",
    )(x, g.reshape(1, k), w)


def _matmul_res_kernel(a_ref, w_ref, r_ref, o_ref):
    o_ref[...] = r_ref[...] + jnp.dot(a_ref[...], w_ref[...], preferred_element_type=F32)


def matmul_res(a, w, res, *, tm, tn):
    m, k = a.shape
    n = w.shape[1]
    assert m % tm == 0 and n % tn == 0
    w_mode = dict(pipeline_mode=pl.Buffered(1)) if tn == n else {}
    return pl.pallas_call(
        _matmul_res_kernel,
        grid=(m // tm, n // tn),
        in_specs=[pl.BlockSpec((tm, k), lambda i, j: (i, 0)),
                  pl.BlockSpec((k, tn), lambda i, j: (0, j), **w_mode),
                  pl.BlockSpec((tm, tn), lambda i, j: (i, j))],
        out_specs=pl.BlockSpec((tm, tn), lambda i, j: (i, j)),
        out_shape=jax.ShapeDtypeStruct((m, n), F32),
        compiler_params=_cparams(2),
        name="matmul_res",
    )(a, w, res)


def _gdn_kernel(q_ref, k_ref, v_ref, z_ref, sm_ref, cwq_ref, cwk_ref, cwv_ref, alog_ref, dtb_ref,
                gn_ref, o_ref,
                s_ref, pq_ref, pk_ref, pv_ref, qs_ref, ks_ref, vs_ref,
                u_ref, w_ref, qd_ref, kt_ref, a_ref, gl_ref, *, heads):
    head0 = pl.program_id(1) * heads
    tb = q_ref.shape[0]
    n_chunks = tb // CHUNK
    hd = HEAD_DIM

    @pl.when(pl.program_id(2) == 0)
    def _():
        s_ref[...] = jnp.zeros_like(s_ref)
        pq_ref[...] = jnp.zeros_like(pq_ref)
        pk_ref[...] = jnp.zeros_like(pk_ref)
        pv_ref[...] = jnp.zeros_like(pv_ref)

    def conv_silu(x_ref, p_ref, cw_ref):
        x = x_ref[...]
        y = _silu(_causal_conv(x, p_ref[...], cw_ref, GDN_CONV))
        p_ref[...] = x[tb - 8:tb, :]
        return y

    q = conv_silu(q_ref, pq_ref, cwq_ref)
    k = conv_silu(k_ref, pk_ref, cwk_ref)
    vs_ref[...] = conv_silu(v_ref, pv_ref, cwv_ref)
    for hh in range(heads):
        cs = slice(hh * hd, (hh + 1) * hd)
        qh = q[:, cs]
        kh = k[:, cs]
        qs_ref[:, cs] = qh * lax.rsqrt(jnp.sum(qh * qh, axis=-1, keepdims=True) + EPS) * (hd ** -0.5)
        ks_ref[:, cs] = kh * lax.rsqrt(jnp.sum(kh * kh, axis=-1, keepdims=True) + EPS)

    r, c = _tri_masks(CHUNK)
    incl = c <= r
    strict = c < r
    tril01 = jnp.where(incl, 1.0, 0.0).astype(BF16)
    eye = jnp.where(r == c, 1.0, 0.0).astype(F32)
    lane = lax.broadcasted_iota(jnp.int32, (1, SMALL_W), 1)
    oh_a = [jnp.where(lane == LANE_A + head0 + hh, 1.0, 0.0).astype(F32) for hh in range(heads)]
    oh_b = [jnp.where(lane == LANE_BETA + head0 + hh, 1.0, 0.0).astype(F32) for hh in range(heads)]
    neg_a = -jnp.exp(alog_ref[...])
    dtb = dtb_ref[...]

    def prep(it, carry):
        rows, rows8, cum, beta_full = [], [], [], []
        for cj in range(GDN_PREP_CHUNKS):
            ci = it * GDN_PREP_CHUNKS + cj
            rows.append(pl.ds(pl.multiple_of(ci * CHUNK, CHUNK), CHUNK))
            rows8.append(pl.ds(pl.multiple_of(ci * 8, 8), 8))
            sm = sm_ref[rows[cj], :]
            g_full = neg_a * _softplus(sm + dtb)
            cum.append(_dot_exact_a(tril01, g_full))
            beta_full.append(_sigmoid(sm))
        units = [(cj, hh) for cj in range(GDN_PREP_CHUNKS) for hh in range(heads)]
        us = range(len(units))
        cs = [slice(hh * hd, (hh + 1) * hd) for _, hh in units]
        qc = [qs_ref[rows[cj], cs[i]] for i, (cj, hh) in enumerate(units)]
        kc = [ks_ref[rows[cj], cs[i]] for i, (cj, hh) in enumerate(units)]
        vc = [vs_ref[rows[cj], cs[i]] for i, (cj, hh) in enumerate(units)]
        g_col = [jnp.sum(cum[cj] * oh_a[hh], axis=-1, keepdims=True) for cj, hh in units]
        oh8 = [jnp.broadcast_to(oh_a[hh], (8, SMALL_W)).astype(BF16) for hh in range(heads)]
        g_row = [_dot_exact_a(oh8[hh], cum[cj], _NT)[0:1, :] for cj, hh in units]
        beta = [jnp.sum(beta_full[cj] * oh_b[hh], axis=-1, keepdims=True) for cj, hh in units]
        decay = [jnp.exp(jnp.where(incl, g_col[i] - g_row[i], -jnp.inf)) for i in us]
        kb = [kc[i] * beta[i] for i in us]
        kk = [_dot(kb[i], kc[i], _NT) for i in us]
        qk = [_dot(qc[i], kc[i], _NT) for i in us]
        p = [-jnp.where(strict, kk[i] * decay[i], 0.0) for i in us]
        t = [eye + p[i] for i in us]
        for _ in range(5):
            p = _dot_hi_many(p, p)
            t = [t[i] + d for i, d in zip(us, _dot_hi_many(t, p))]
        eg = [jnp.exp(g_col[i]) for i in us]
        g_last = [g_col[i][CHUNK - 1:CHUNK, :] for i in us]
        u = [_dot(t[i], vc[i] * beta[i]) for i in us]
        w = [_dot(t[i], kb[i] * eg[i]) for i in us]
        for i, (cj, hh) in enumerate(units):
            u_ref[rows[cj], cs[i]] = u[i]
            w_ref[rows[cj], cs[i]] = w[i]
            a_ref[hh, rows[cj], :] = qk[i] * decay[i]
            qd_ref[rows[cj], cs[i]] = qc[i] * eg[i]
            kt_ref[rows[cj], cs[i]] = kc[i] * jnp.exp(g_last[i] - g_col[i])
            gl_ref[hh, rows8[cj], :] = jnp.broadcast_to(jnp.exp(g_last[i]), (8, hd))
        return carry

    lax.fori_loop(0, n_chunks // GDN_PREP_CHUNKS, prep, 0)

    gn = gn_ref[...]

    def scan(ci, carry):
        rows = pl.ds(pl.multiple_of(ci * CHUNK, CHUNK), CHUNK)
        hs = range(heads)
        cs = [slice(hh * hd, (hh + 1) * hd) for hh in hs]
        s = [s_ref[hh] for hh in hs]
        sb = [s[hh].astype(BF16) for hh in hs]
        ws = [_dot(w_ref[rows, cs[hh]], sb[hh]) for hh in hs]
        qs = [_dot(qd_ref[rows, cs[hh]], sb[hh]) for hh in hs]
        vb = [(u_ref[rows, cs[hh]] - ws[hh]).astype(BF16) for hh in hs]
        av = [_dot(a_ref[hh, rows, :], vb[hh]) for hh in hs]
        kv = [_dot(kt_ref[rows, cs[hh]], vb[hh], _TN) for hh in hs]
        for hh in hs:
            gl = gl_ref[hh, pl.ds(pl.multiple_of(ci * 8, 8), 1), :]
            s_ref[hh] = s[hh] * gl + kv[hh]
            y = _rms_rows(qs[hh] + av[hh], gn) * _silu(z_ref[rows, cs[hh]])
            o_ref[rows, cs[hh]] = y.astype(o_ref.dtype)
        return carry

    lax.fori_loop(0, n_chunks, scan, 0)


def gdn_mixer(proj, conv_w, alog_vec, dtb_vec, gnorm, *, tb, heads):
    b, s, _ = proj.shape
    assert s % tb == 0 and tb % CHUNK == 0 and GDN_HEADS % heads == 0
    hd = HEAD_DIM
    gw = heads * hd
    assert all(off % gw == 0 for off in (OFF_QKV_A, GDN_W, OFF_Z_A))
    col = lambda off: (lambda bi, gi, ti: (bi, ti, off // gw + gi))
    cwcol = lambda off: (lambda bi, gi, ti: (0, off // gw + gi))
    const = lambda bi, gi, ti: (0, 0)
    blk = (None, tb, gw)
    return pl.pallas_call(
        functools.partial(_gdn_kernel, heads=heads),
        grid=(b, GDN_HEADS // heads, s // tb),
        in_specs=[pl.BlockSpec(blk, col(OFF_QKV_A)),
                  pl.BlockSpec(blk, col(OFF_QKV_A + GDN_W)),
                  pl.BlockSpec(blk, col(OFF_QKV_A + 2 * GDN_W)),
                  pl.BlockSpec(blk, col(OFF_Z_A)),
                  pl.BlockSpec((None, tb, SMALL_W), lambda bi, gi, ti: (bi, ti, OFF_SMALL // SMALL_W)),
                  pl.BlockSpec((GDN_CONV, gw), cwcol(0)),
                  pl.BlockSpec((GDN_CONV, gw), cwcol(GDN_W)),
                  pl.BlockSpec((GDN_CONV, gw), cwcol(2 * GDN_W)),
                  pl.BlockSpec((1, SMALL_W), const),
                  pl.BlockSpec((1, SMALL_W), const),
                  pl.BlockSpec((1, hd), const)],
        out_specs=pl.BlockSpec(blk, lambda bi, gi, ti: (bi, ti, gi)),
        out_shape=jax.ShapeDtypeStruct((b, s, GDN_W), BF16),
        scratch_shapes=[pltpu.VMEM((heads, hd, hd), F32)]
                       + [pltpu.VMEM((8, gw), F32)] * 3
                       + [pltpu.VMEM((tb, gw), F32)] * 3
                       + [pltpu.VMEM((tb, gw), F32)] * 4
                       + [pltpu.VMEM((heads, tb, CHUNK), F32),
                          pltpu.VMEM((heads, tb // CHUNK * 8, hd), F32)],
        compiler_params=_cparams(3),
        name="gdn_mixer",
    )(proj, proj, proj, proj, proj, conv_w, conv_w, conv_w, alog_vec, dtb_vec, gnorm.reshape(1, hd))


def _mlstm_kernel(q_ref, k_ref, v_ref, og_ref, sm_ref, ib_ref, fb_ref, gn_ref, o_ref,
                  c_ref, n_ref, m_ref, *, heads):
    head0 = pl.program_id(1) * heads
    tb = q_ref.shape[0]
    n_chunks = tb // CHUNK
    hd = HEAD_DIM

    @pl.when(pl.program_id(2) == 0)
    def _():
        c_ref[...] = jnp.zeros_like(c_ref)
        n_ref[...] = jnp.zeros_like(n_ref)
        m_ref[...] = jnp.zeros_like(m_ref)

    r, c = _tri_masks(CHUNK)
    incl = c <= r
    tril01 = jnp.where(incl, 1.0, 0.0).astype(BF16)
    lane = lax.broadcasted_iota(jnp.int32, (1, SMALL_W), 1)
    oh_i = [jnp.where(lane == LANE_I + head0 + hh, 1.0, 0.0).astype(F32) for hh in range(heads)]
    oh_f = [jnp.where(lane == LANE_F + head0 + hh, 1.0, 0.0).astype(F32) for hh in range(heads)]
    ib = ib_ref[...]
    fb = fb_ref[...]
    gn = gn_ref[...]

    hs = range(heads)
    cs = [slice(hh * hd, (hh + 1) * hd) for hh in hs]
    oh_f8 = [jnp.broadcast_to(oh_f[hh], (8, SMALL_W)).astype(BF16) for hh in hs]
    oh_i8 = [jnp.broadcast_to(oh_i[hh], (8, SMALL_W)).astype(BF16) for hh in hs]

    def state_free(ci):
        rows = pl.ds(pl.multiple_of(ci * CHUNK, CHUNK), CHUNK)
        sm = sm_ref[rows, :]
        i_full = sm + ib
        b_full = _dot_exact_a(tril01, _log_sigmoid(sm + fb))
        qc = [q_ref[rows, cs[hh]] * (hd ** -0.5) for hh in hs]
        kc = [k_ref[rows, cs[hh]] for hh in hs]
        vc = [v_ref[rows, cs[hh]] for hh in hs]
        s_qk = [_dot(qc[hh], kc[hh], _NT) for hh in hs]
        b_col = [jnp.sum(b_full * oh_f[hh], axis=-1, keepdims=True) for hh in hs]
        i_col = [jnp.sum(i_full * oh_i[hh], axis=-1, keepdims=True) for hh in hs]
        b_row = [_dot_exact_a(oh_f8[hh], b_full, _NT)[0:1, :] for hh in hs]
        i_row = [_dot_exact_a(oh_i8[hh], i_full, _NT)[0:1, :] for hh in hs]
        dmat = [jnp.where(incl, b_col[hh] - b_row[hh] + i_row[hh], -jnp.inf) for hh in hs]
        d_max = [jnp.max(dmat[hh], axis=-1, keepdims=True) for hh in hs]
        b_last = [b_col[hh][CHUNK - 1:CHUNK, :] for hh in hs]
        a_end = [b_last[hh] - b_col[hh] + i_col[hh] for hh in hs]
        a_max = [jnp.max(a_end[hh], axis=0, keepdims=True) for hh in hs]
        return rows, qc, kc, vc, s_qk, b_col, dmat, d_max, b_last, a_end, a_max

    def advance(pre, state):
        rows, qc, kc, vc, s_qk, b_col, dmat, d_max, b_last, a_end, a_max = pre
        cmat, nvec, m = state
        q_c = [_dot(qc[hh], cmat[hh]) for hh in hs]
        inter = [b_col[hh] + m[hh] for hh in hs]
        m_t = [jnp.maximum(inter[hh], d_max[hh]) for hh in hs]
        w_inter = [jnp.exp(inter[hh] - m_t[hh]) for hh in hs]
        wmat = [jnp.exp(dmat[hh] - m_t[hh]) * s_qk[hh] for hh in hs]
        m_new = [jnp.maximum(b_last[hh] + m[hh], a_max[hh]) for hh in hs]
        w_old = [jnp.exp(b_last[hh] + m[hh] - m_new[hh]) for hh in hs]
        kw = [kc[hh] * jnp.exp(a_end[hh] - m_new[hh]) for hh in hs]
        w_v = [_dot(wmat[hh], vc[hh]) for hh in hs]
        k_v = [_dot(kw[hh], vc[hh], _TN) for hh in hs]
        c_new, n_new = [], []
        for hh in hs:
            num = w_inter[hh] * q_c[hh] + w_v[hh]
            den = (w_inter[hh] * jnp.sum(qc[hh] * nvec[hh], axis=-1, keepdims=True)
                   + jnp.sum(wmat[hh], axis=-1, keepdims=True))
            hout = num / jnp.maximum(jnp.abs(den), jnp.exp(-m_t[hh]))
            c_new.append(w_old[hh] * cmat[hh] + k_v[hh])
            n_new.append(w_old[hh] * nvec[hh] + jnp.sum(kw[hh], axis=0, keepdims=True))
            y = _rms_rows(hout, gn) * _sigmoid(og_ref[rows, cs[hh]])
            o_ref[rows, cs[hh]] = y.astype(o_ref.dtype)
        return c_new, n_new, m_new

    def chunk_group(it, carry):
        pre = [state_free(it * ML_CHUNK_GROUP + cj) for cj in range(ML_CHUNK_GROUP)]
        state = ([c_ref[hh] for hh in hs], [n_ref[hh] for hh in hs],
                 [m_ref[hh, 0:1, 0:1] for hh in hs])
        for cj in range(ML_CHUNK_GROUP):
            state = advance(pre[cj], state)
        for hh in hs:
            c_ref[hh] = state[0][hh]
            n_ref[hh] = state[1][hh]
            m_ref[hh] = jnp.broadcast_to(state[2][hh], m_ref.shape[1:])
        return carry

    lax.fori_loop(0, n_chunks // ML_CHUNK_GROUP, chunk_group, 0)


def mlstm_mixer(proj, ib_vec, fb_vec, gnorm, *, tb, heads):
    b, s, _ = proj.shape
    assert s % tb == 0 and tb % CHUNK == 0 and ML_HEADS % heads == 0
    hd = HEAD_DIM
    gw = heads * hd
    assert all(off % gw == 0 for off in (OFF_QB, OFF_KB, OFF_VB, OFF_OB))
    col = lambda off: (lambda bi, gi, ti: (bi, ti, off // gw + gi))
    const = lambda bi, gi, ti: (0, 0)
    blk = (None, tb, gw)
    return pl.pallas_call(
        functools.partial(_mlstm_kernel, heads=heads),
        grid=(b, ML_HEADS // heads, s // tb),
        in_specs=[pl.BlockSpec(blk, col(OFF_QB)),
                  pl.BlockSpec(blk, col(OFF_KB)),
                  pl.BlockSpec(blk, col(OFF_VB)),
                  pl.BlockSpec(blk, col(OFF_OB)),
                  pl.BlockSpec((None, tb, SMALL_W), lambda bi, gi, ti: (bi, ti, OFF_SMALL // SMALL_W)),
                  pl.BlockSpec((1, SMALL_W), const),
                  pl.BlockSpec((1, SMALL_W), const),
                  pl.BlockSpec((1, hd), const)],
        out_specs=pl.BlockSpec(blk, lambda bi, gi, ti: (bi, ti, gi)),
        out_shape=jax.ShapeDtypeStruct((b, s, ML_W), BF16),
        scratch_shapes=[pltpu.VMEM((heads, hd, hd), F32),
                        pltpu.VMEM((heads, 1, hd), F32),
                        pltpu.VMEM((heads, 8, hd), F32)],
        compiler_params=_cparams(3),
        name="mlstm_mixer",
    )(proj, proj, proj, proj, proj, ib_vec, fb_vec, gnorm.reshape(1, hd))


def _sb_kernel(q_ref, k_ref, v_ref, o_ref, acc_ref, carry_ref, *, heads):
    tq = q_ref.shape[0]
    hd = HEAD_DIM
    qi = pl.program_id(2)
    hs = range(heads)
    cs = [slice(hh * hd, (hh + 1) * hd) for hh in hs]
    qb = [(q_ref[:, cs[hh]] * (hd ** -0.5)).astype(BF16) for hh in hs]
    r, c = _tri_masks(tq)
    strict = c < r
    later01 = jnp.where(r > c, 1.0, 0.0).astype(BF16)

    acc_ref[...] = jnp.zeros_like(acc_ref)
    carry_ref[...] = jnp.zeros_like(carry_ref)

    def process(kb, diag):
        rows = pl.ds(pl.multiple_of(kb * tq, tq), tq)
        d = functools.partial(lax.dot_general, preferred_element_type=F32)
        z = [d(qb[hh], k_ref[rows, cs[hh]].astype(BF16), _NT) for hh in hs]
        sp = [jnp.log(1.0 + jnp.exp(-jnp.abs(z[hh]))) for hh in hs]
        log_beta = [jnp.minimum(z[hh], 0.0) - sp[hh] for hh in hs]
        log_stay = [jnp.minimum(-z[hh], 0.0) - sp[hh] for hh in hs]
        if diag:
            log_stay = [jnp.where(strict, ls, 0.0) for ls in log_stay]
        parts = [_split_bf16(ls, 2) for ls in log_stay]
        in_hi = [d(parts[hh][0], later01, _NN) for hh in hs]
        in_lo = [d(parts[hh][1], later01, _NN) for hh in hs]
        inblk = [in_hi[hh] + in_lo[hh] for hh in hs]
        carry = [carry_ref[hh] for hh in hs]
        a = [jnp.exp(log_beta[hh] + inblk[hh] + carry[hh]) for hh in hs]
        if diag:
            a = [jnp.where(strict, x, 0.0) for x in a]
        av = [_dot(a[hh], v_ref[rows, cs[hh]]) for hh in hs]
        top = None
        for hh in hs:
            acc_ref[:, cs[hh]] += av[hh]
            new_carry = carry[hh] + inblk[hh][:, 0:1] + log_stay[hh][:, 0:1]
            carry_ref[hh] = new_carry
            mx = jnp.max(new_carry)
            top = mx if top is None else jnp.maximum(top, mx)
        return top

    top = process(qi, True)

    def cond(st):
        return jnp.logical_and(st[0] >= 0, st[1] >= SB_DEAD_LOG)

    def body(st):
        return st[0] - 1, process(st[0], False)

    lax.while_loop(cond, body, (qi - 1, top))
    o_ref[...] = acc_ref[...].astype(o_ref.dtype)


def sb_mixer(proj, *, tq, heads):
    b, s, _ = proj.shape
    assert s % tq == 0 and SB_HEADS % heads == 0
    gw = heads * HEAD_DIM
    assert all(off % gw == 0 for off in (OFF_QC, OFF_KC, OFF_VC))
    resident = dict(pipeline_mode=pl.Buffered(1))
    return pl.pallas_call(
        functools.partial(_sb_kernel, heads=heads),
        grid=(b, SB_HEADS // heads, s // tq),
        in_specs=[pl.BlockSpec((None, tq, gw), lambda bi, gi, ti: (bi, ti, OFF_QC // gw + gi)),
                  pl.BlockSpec((None, s, gw), lambda bi, gi, ti: (bi, 0, OFF_KC // gw + gi), **resident),
                  pl.BlockSpec((None, s, gw), lambda bi, gi, ti: (bi, 0, OFF_VC // gw + gi), **resident)],
        out_specs=pl.BlockSpec((None, tq, gw), lambda bi, gi, ti: (bi, ti, gi)),
        out_shape=jax.ShapeDtypeStruct((b, s, SB_W), BF16),
        scratch_shapes=[pltpu.VMEM((tq, gw), F32), pltpu.VMEM((heads, tq, 1), F32)],
        compiler_params=_cparams(3),
        name="sb_mixer",
    )(proj, proj, proj)


def _merge_kernel(ya_ref, yb_ref, yc_ref, wa_ref, wb_ref, wc_ref, ga_ref, gb_ref, gc_ref, o_ref):
    d = functools.partial(jnp.dot, preferred_element_type=F32)
    m = (_sigmoid(ga_ref[...]) * d(ya_ref[...], wa_ref[...])
         + _sigmoid(gb_ref[...]) * d(yb_ref[...], wb_ref[...])
         + _sigmoid(gc_ref[...]) * d(yc_ref[...], wc_ref[...]))
    o_ref[...] = m.astype(o_ref.dtype)


def merge_branches(ya, yb, yc, wa, wb, wc, proj2d, *, tm, tn):
    m = ya.shape[0]
    n = wa.shape[1]
    assert m % tm == 0 and n % tn == 0
    gate = lambda which: (lambda i, j: (i, (OFF_GATE + which * D_MODEL) // tn + j))
    row = lambda i, j: (i, 0)
    wcol = lambda i, j: (0, j)
    return pl.pallas_call(
        _merge_kernel,
        grid=(m // tm, n // tn),
        in_specs=[pl.BlockSpec((tm, ya.shape[1]), row),
                  pl.BlockSpec((tm, yb.shape[1]), row),
                  pl.BlockSpec((tm, yc.shape[1]), row),
                  pl.BlockSpec((wa.shape[0], tn), wcol),
                  pl.BlockSpec((wb.shape[0], tn), wcol),
                  pl.BlockSpec((wc.shape[0], tn), wcol),
                  pl.BlockSpec((tm, tn), gate(0)),
                  pl.BlockSpec((tm, tn), gate(1)),
                  pl.BlockSpec((tm, tn), gate(2))],
        out_specs=pl.BlockSpec((tm, tn), lambda i, j: (i, j)),
        out_shape=jax.ShapeDtypeStruct((m, n), BF16),
        compiler_params=_cparams(2),
        name="merge_branches",
    )(ya, yb, yc, wa, wb, wc, proj2d, proj2d, proj2d)


def _xattn_kernel(q_ref, kv_ref, gq_ref, gk_ref, o_ref):
    gq = gq_ref[...]
    gk = gk_ref[...]
    for hh in range(XA_HEADS):
        cs = slice(hh * XA_DH, (hh + 1) * XA_DH)
        qh = _rms_rows(q_ref[:, cs], gq)
        kh = _rms_rows(kv_ref[:, cs], gk)
        vh = kv_ref[:, D_MODEL + hh * XA_DH:D_MODEL + (hh + 1) * XA_DH]
        logits = _dot(qh, kh, _NT) * (XA_DH ** -0.5)
        mx = jnp.max(logits, axis=-1, keepdims=True)
        e = jnp.exp(logits - mx)
        p = e / jnp.sum(e, axis=-1, keepdims=True)
        o_ref[:, cs] = _dot(p, vh).astype(o_ref.dtype)


def cross_attention(q, kv, gq, gk, *, tm):
    b, s, d = q.shape
    mlen = kv.shape[1]
    assert s % tm == 0
    const = lambda bi, ti: (0, 0)
    return pl.pallas_call(
        _xattn_kernel,
        grid=(b, s // tm),
        in_specs=[pl.BlockSpec((None, tm, d), lambda bi, ti: (bi, ti, 0)),
                  pl.BlockSpec((None, mlen, 2 * d), lambda bi, ti: (bi, 0, 0)),
                  pl.BlockSpec((1, XA_DH), const),
                  pl.BlockSpec((1, XA_DH), const)],
        out_specs=pl.BlockSpec((None, tm, d), lambda bi, ti: (bi, ti, 0)),
        out_shape=jax.ShapeDtypeStruct((b, s, d), BF16),
        compiler_params=_cparams(2),
        name="cross_attention",
    )(q, kv, gq.reshape(1, XA_DH), gk.reshape(1, XA_DH))


def _ffn_up_kernel(x_ref, gn_ref, wg_ref, wu_ref, cg_ref, cu_ref, o_ref,
                   xn_ref, pg_ref, pu_ref, *, blocks_per_seq):
    i = pl.program_id(0)
    j = pl.program_id(1)
    tm = x_ref.shape[0]

    @pl.when(j == 0)
    def _():
        xn_ref[...] = _rms_rows(x_ref[...], gn_ref[...]).astype(BF16)

    @pl.when(i % blocks_per_seq == 0)
    def _():
        pg_ref[j] = jnp.zeros(pg_ref.shape[1:], F32)
        pu_ref[j] = jnp.zeros(pu_ref.shape[1:], F32)

    xn = xn_ref[...]
    g = jnp.dot(xn, wg_ref[...], preferred_element_type=F32)
    u = jnp.dot(xn, wu_ref[...], preferred_element_type=F32)
    gc = _causal_conv(g, pg_ref[j], cg_ref, FFN_CONV)
    uc = _causal_conv(u, pu_ref[j], cu_ref, FFN_CONV)
    pg_ref[j] = g[tm - 8:tm, :]
    pu_ref[j] = u[tm - 8:tm, :]
    o_ref[...] = (_silu(gc) * uc).astype(o_ref.dtype)


def ffn_up_glu(x, gn, w_up, w_conv, *, tm, tn, seq_len):
    m, k = x.shape
    assert m % tm == 0 and D_FF % tn == 0 and seq_len % tm == 0
    nj = D_FF // tn
    kern = functools.partial(_ffn_up_kernel, blocks_per_seq=seq_len // tm)
    return pl.pallas_call(
        kern,
        grid=(m // tm, nj),
        in_specs=[pl.BlockSpec((tm, k), lambda i, j: (i, 0)),
                  pl.BlockSpec((1, k), lambda i, j: (0, 0)),
                  pl.BlockSpec((k, tn), lambda i, j: (0, j)),
                  pl.BlockSpec((k, tn), lambda i, j: (0, nj + j)),
                  pl.BlockSpec((FFN_CONV, tn), lambda i, j: (0, j)),
                  pl.BlockSpec((FFN_CONV, tn), lambda i, j: (0, nj + j))],
        out_specs=pl.BlockSpec((tm, tn), lambda i, j: (i, j)),
        out_shape=jax.ShapeDtypeStruct((m, D_FF), BF16),
        scratch_shapes=[pltpu.VMEM((tm, k), BF16),
                        pltpu.VMEM((nj, 8, tn), F32),
                        pltpu.VMEM((nj, 8, tn), F32)],
        compiler_params=_cparams(2),
        name="ffn_up",
    )(x, gn.reshape(1, k), w_up, w_up, w_conv, w_conv)


def _pad_lanes(v, lane0, width):
    out = jnp.zeros((1, width), F32)
    return lax.dynamic_update_slice(out, v.reshape(1, -1).astype(F32), (0, lane0))


def _arrange_w_in_t(w):
    sizes = (3 * GDN_W, GDN_HEADS, GDN_HEADS, GDN_W, ML_W, ML_W, ML_W, ML_HEADS, ML_HEADS, ML_W,
             SB_W, SB_W, SB_W, 3 * D_MODEL)
    offs = [0]
    for sz in sizes:
        offs.append(offs[-1] + sz)
    wt = w.T
    part = lambda i: wt[offs[i]:offs[i + 1]]
    (qkv_a, a_pre, b_pre, z_a, q_b, k_b, v_b, i_b, f_b, o_b, q_c, k_c, v_c, gate) = (
        part(i) for i in range(len(sizes)))
    small = jnp.concatenate([a_pre, b_pre, i_b, f_b], axis=0)
    small = jnp.pad(small, ((0, SMALL_W - small.shape[0]), (0, 0)))
    return jnp.concatenate([gate, qkv_a, z_a, q_b, k_b, v_b, o_b, q_c, k_c, v_c, small],
                           axis=0).astype(BF16)


def _tiles(seq):
    return dict(
        in_proj=dict(tm=min(1024, seq), tn=1536),
        square=dict(tm=min(512, seq), tn=D_MODEL),
        mem_kv=dict(tm=512, tn=1024),
        ffn_up=dict(tm=min(1024, seq), tn=512),
        ffn_down=dict(tm=min(256, seq), tn=D_MODEL),
        merge=dict(tm=min(256, seq), tn=D_MODEL),
        xattn=dict(tm=min(512, seq)),
        gdn=dict(tb=min(1024, seq), heads=3),
        mlstm=dict(tb=min(1024, seq), heads=4),
        sb=dict(tq=min(256, seq), heads=2),
    )


def _layer(x, mem2d, p, *, batch, seq, mem_len):
    n = batch * seq
    t = _tiles(seq)
    proj = rms_matmul(x, p["norm_mix"], p["w_in_t"], w_transposed=True, **t["in_proj"])
    proj3 = proj.reshape(batch, seq, PROJ_W)
    ya = gdn_mixer(proj3, p["gdn_conv"], p["alog_vec"], p["dtb_vec"], p["gdn_norm"], **t["gdn"])
    yb = mlstm_mixer(proj3, p["ib_vec"], p["fb_vec"], p["ml_norm"], **t["mlstm"])
    yc = sb_mixer(proj3, **t["sb"])
    merged = merge_branches(ya.reshape(n, GDN_W), yb.reshape(n, ML_W), yc.reshape(n, SB_W),
                            p["wb_a"], p["wb_b"], p["wb_c"], proj, **t["merge"])
    x = matmul_res(merged, p["w_out"], x, **t["square"])
    q = rms_matmul(x, p["norm_xa"], p["xa_wq"], **t["square"])
    kv = rms_matmul(mem2d, p["norm_mem"], p["xa_wkv"], **t["mem_kv"])
    o = cross_attention(q.reshape(batch, seq, D_MODEL), kv.reshape(batch, mem_len, 2 * D_MODEL),
                        p["xa_qnorm"], p["xa_knorm"], **t["xattn"])
    x = matmul_res(o.reshape(n, D_MODEL), p["xa_wo"], x, **t["square"])
    act = ffn_up_glu(x, p["norm_ffn"], p["ffn_up"], p["ffn_conv"], seq_len=seq, **t["ffn_up"])
    x = matmul_res(act, p["ffn_down"], x, **t["ffn_down"])
    return x


def kernel(x, mem, norm_mix, w_in, gdn_conv, gdn_a_log, gdn_dt_bias, gdn_norm, ml_gate_bias, ml_norm, w_br, w_out, norm_xa, norm_mem, xa_wq, xa_wkv, xa_wo, xa_qnorm, xa_knorm, norm_ffn, ffn_up, ffn_conv, ffn_down):
    batch, seq, d = x.shape
    mem_len = mem.shape[1]
    depth = w_in.shape[0]
    h = x.reshape(batch * seq, d)
    mem2d = mem.reshape(batch * mem_len, d)
    for l in range(depth):
        p = {
            "norm_mix": norm_mix[l],
            "w_in_t": _arrange_w_in_t(w_in[l]),
            "gdn_conv": gdn_conv[l],
            "alog_vec": _pad_lanes(gdn_a_log[l], LANE_A, SMALL_W),
            "dtb_vec": _pad_lanes(gdn_dt_bias[l], LANE_A, SMALL_W),
            "gdn_norm": gdn_norm[l],
            "ib_vec": _pad_lanes(ml_gate_bias[l, 0], LANE_I, SMALL_W),
            "fb_vec": _pad_lanes(ml_gate_bias[l, 1], LANE_F, SMALL_W),
            "ml_norm": ml_norm[l],
            "wb_a": w_br[l, :GDN_W].astype(BF16),
            "wb_b": w_br[l, GDN_W:GDN_W + ML_W].astype(BF16),
            "wb_c": w_br[l, GDN_W + ML_W:].astype(BF16),
            "w_out": w_out[l].astype(BF16),
            "norm_xa": norm_xa[l],
            "norm_mem": norm_mem[l],
            "xa_wq": xa_wq[l].astype(BF16),
            "xa_wkv": xa_wkv[l].astype(BF16),
            "xa_wo": xa_wo[l].astype(BF16),
            "xa_qnorm": xa_qnorm[l],
            "xa_knorm": xa_knorm[l],
            "norm_ffn": norm_ffn[l],
            "ffn_up": ffn_up[l].astype(BF16),
            "ffn_conv": ffn_conv[l],
            "ffn_down": ffn_down[l].astype(BF16),
        }
        h = _layer(h, mem2d, p, batch=batch, seq=seq, mem_len=mem_len)
    return h.reshape(batch, seq, d)
```
